```python
import jax, jax.numpy as jnp
from jax import lax
import numpy as np

D_MODEL = 1024
BATCH = 32
SEQ = 2048
DEPTH = 1
DEC_BATCH = 4
DEC_SEQ = 4096
PAST_LEN = 128

GRID_W = 64
N_ATT_HEADS = 8
ATT_HD = 64
ATT_W = N_ATT_HEADS * ATT_HD
WIN_R = 8
WIN_C = 16
N_DN_HEADS = 4
DN_HD = 128
DN_W = N_DN_HEADS * DN_HD
DN_CONV = 3
CHUNK = 64
N_DIR = 2
D_FF = 2816
FFN_CONV = 3
IN_COLS = 3 * ATT_W + 4 * DN_W + 2 * N_DIR * N_DN_HEADS
NORM_EPS = 1e-6

kernel_name = "hybrid_natten_gdn_encoder"


def _rmsnorm(x, g):
    xf = x.astype(jnp.float32)
    y = xf * lax.rsqrt(jnp.mean(xf * xf, axis=-1, keepdims=True) + NORM_EPS)
    return (y * g.astype(jnp.float32)).astype(x.dtype)


def _l2norm(x):
    return x * lax.rsqrt(jnp.sum(x * x, axis=-1, keepdims=True) + NORM_EPS)


def _dwconv_centred(x, w):
    K = w.shape[0]
    pad = K // 2
    T = x.shape[1]
    xp = jnp.pad(x, ((0, 0), (pad, pad), (0, 0)))
    return sum(xp[:, j:j + T] * w[j] for j in range(K))


def _neighbourhood_attention(q, k, v, rpb):
    B, T, H, Dh = q.shape
    rows = T // GRID_W
    kr = min(WIN_R, rows)
    qg = q.reshape(B, rows, GRID_W, H, Dh)
    kg = k.reshape(B, rows, GRID_W, H, Dh)
    vg = v.reshape(B, rows, GRID_W, H, Dh)
    col = jnp.arange(GRID_W)
    col_idx = jnp.clip(col - WIN_C // 2, 0, GRID_W - WIN_C)[:, None] + jnp.arange(WIN_C)
    col_bias = rpb[:, :, col_idx - col[:, None] + WIN_C - 1]
    scale = Dh ** -0.5

    def one_row(r):
        r0 = jnp.clip(r - kr // 2, 0, rows - kr)
        q_r = lax.dynamic_index_in_dim(qg, r, axis=1, keepdims=False)
        k_band = lax.dynamic_slice_in_dim(kg, r0, kr, axis=1)
        v_band = lax.dynamic_slice_in_dim(vg, r0, kr, axis=1)
        k_sel = k_band[:, :, col_idx]
        v_sel = v_band[:, :, col_idx]
        s = jnp.einsum('bqhd,brqjhd->bhqrj', q_r, k_sel).astype(jnp.float32) * scale
        row_off = r0 + jnp.arange(kr) - r + WIN_R - 1
        bias = jnp.transpose(col_bias[:, row_off], (0, 2, 1, 3))
        s = s + bias.astype(jnp.float32)
        p = jax.nn.softmax(s.reshape(B, H, GRID_W, kr * WIN_C), axis=-1).reshape(s.shape)
        return jnp.einsum('bhqrj,brqjhd->bqhd', p.astype(v.dtype), v_sel)

    out = lax.map(one_row, jnp.arange(rows))
    return jnp.moveaxis(out, 0, 1).reshape(B, T, H * Dh)


def _gated_delta_chunked(q, k, v, g, beta):
    B, T, H, Dk = q.shape
    Dv = v.shape[-1]
    N = T // CHUNK

    def to_chunks(a):
        a = a.reshape((B, N, CHUNK, H) + a.shape[3:])
        return jnp.moveaxis(a, 3, 1)

    q = to_chunks(q * (Dk ** -0.5))
    k = to_chunks(k)
    v = to_chunks(v)
    g = to_chunks(g)
    beta = to_chunks(beta)
    gc = jnp.cumsum(g, axis=-1)
    causal = jnp.tril(jnp.ones((CHUNK, CHUNK), bool))
    strict = jnp.tril(jnp.ones((CHUNK, CHUNK), bool), -1)
    decay = jnp.exp(jnp.where(causal, gc[..., :, None] - gc[..., None, :], -jnp.inf))
    kb = k * beta[..., None]
    lower = jnp.where(strict, jnp.einsum('bhncd,bhnsd->bhncs', kb, k) * decay, 0.0)
    a_mat = lower + jnp.eye(CHUNK, dtype=lower.dtype)
    u = lax.linalg.triangular_solve(a_mat, v * beta[..., None], left_side=True, lower=True, unit_diagonal=True)
    w = lax.linalg.triangular_solve(a_mat, kb * jnp.exp(gc)[..., None], left_side=True, lower=True, unit_diagonal=True)
    intra = jnp.where(causal, jnp.einsum('bhncd,bhnsd->bhncs', q, k) * decay, 0.0)
    g_last = gc[..., -1]
    k_dec = k * jnp.exp(g_last[..., None] - gc)[..., None]
    q_dec = q * jnp.exp(gc)[..., None]

    def step(S, xs):
        q_i, k_i, u_i, w_i, a_i, gl_i = xs
        v_new = u_i - jnp.einsum('bhcd,bhde->bhce', w_i, S)
        o = jnp.einsum('bhcd,bhde->bhce', q_i, S) + jnp.einsum('bhcs,bhse->bhce', a_i, v_new)
        S = S * jnp.exp(gl_i)[..., None, None] + jnp.einsum('bhcd,bhce->bhde', k_i, v_new)
        return S, o

    xs = (jnp.moveaxis(q_dec, 2, 0), jnp.moveaxis(k_dec, 2, 0), jnp.moveaxis(u, 2, 0),
          jnp.moveaxis(w, 2, 0), jnp.moveaxis(intra, 2, 0), jnp.moveaxis(g_last, 2, 0))
    S0 = jnp.zeros((B, H, Dk, Dv), jnp.float32)
    _, o = lax.scan(step, S0, xs)
    o = jnp.moveaxis(o, 0, 2)
    return jnp.moveaxis(o, 1, 3).reshape(B, T, H, Dv)


def _mixer(h, w_in, att_q_norm, att_k_norm, att_rpb, dn_conv_w, dn_a_log, dn_dt_bias, dn_out_norm, w_o):
    B, T, _ = h.shape
    proj = h @ w_in
    s4 = 3 * ATT_W + 4 * DN_W
    aq, ak, av, dqkv, dz, dbeta, dalpha = jnp.split(
        proj, [ATT_W, 2 * ATT_W, 3 * ATT_W, 3 * ATT_W + 3 * DN_W, s4, s4 + N_DIR * N_DN_HEADS], axis=-1)
    aq = _rmsnorm(aq.reshape(B, T, N_ATT_HEADS, ATT_HD), att_q_norm)
    ak = _rmsnorm(ak.reshape(B, T, N_ATT_HEADS, ATT_HD), att_k_norm)
    av = av.reshape(B, T, N_ATT_HEADS, ATT_HD)
    att = _neighbourhood_attention(aq, ak, av, att_rpb)
    dqkv = jax.nn.silu(_dwconv_centred(dqkv, dn_conv_w)).astype(jnp.float32)
    dq, dk, dv = jnp.split(dqkv, 3, axis=-1)
    dq = _l2norm(dq.reshape(B, T, N_DN_HEADS, DN_HD))
    dk = _l2norm(dk.reshape(B, T, N_DN_HEADS, DN_HD))
    dv = dv.reshape(B, T, N_DN_HEADS, DN_HD)
    beta = jax.nn.sigmoid(dbeta.astype(jnp.float32)).reshape(B, T, N_DIR, N_DN_HEADS)
    g = -jnp.exp(dn_a_log.astype(jnp.float32)) * jax.nn.softplus(
        dalpha.astype(jnp.float32).reshape(B, T, N_DIR, N_DN_HEADS) + dn_dt_bias.astype(jnp.float32))
    o_f = _gated_delta_chunked(dq, dk, dv, g[:, :, 0], beta[:, :, 0])
    o_b = jnp.flip(_gated_delta_chunked(jnp.flip(dq, 1), jnp.flip(dk, 1), jnp.flip(dv, 1),
                                        jnp.flip(g[:, :, 1], 1), jnp.flip(beta[:, :, 1], 1)), 1)
    o = _rmsnorm(o_f + o_b, dn_out_norm) * jax.nn.silu(dz.astype(jnp.float32).reshape(B, T, N_DN_HEADS, DN_HD))
    dn = o.reshape(B, T, DN_W).astype(h.dtype)
    return jnp.concatenate([att, dn], axis=-1) @ w_o


def _conv_ffn(h, w_up, conv_w, conv_b, w_down):
    u = _dwconv_centred(h @ w_up, conv_w) + conv_b
    a, b = jnp.split(u, 2, axis=-1)
    return (jax.nn.silu(a) * b) @ w_down


def _trunk(x, c, ada_w, ada_b, norm1_g, norm2_g, w_in, att_q_norm, att_k_norm, att_rpb,
           dn_conv_w, dn_a_log, dn_dt_bias, dn_out_norm, w_o, ffn_w_up, ffn_conv_w, ffn_conv_b, ffn_w_down):
    for l in range(DEPTH):
        mod = (jax.nn.silu(c.astype(jnp.float32)) @ ada_w[l].astype(jnp.float32)
               + ada_b[l].astype(jnp.float32)).astype(x.dtype)
        sh1, sc1, g1, sh2, sc2, g2 = jnp.split(mod[:, None, :], 6, axis=-1)
        h = _rmsnorm(x, norm1_g[l]) * (1 + sc1) + sh1
        x = x + g1 * _mixer(h, w_in[l], att_q_norm[l], att_k_norm[l], att_rpb[l], dn_conv_w[l],
                            dn_a_log[l], dn_dt_bias[l], dn_out_norm[l], w_o[l])
        h = _rmsnorm(x, norm2_g[l]) * (1 + sc2) + sh2
        x = x + g2 * _conv_ffn(h, ffn_w_up[l], ffn_conv_w[l], ffn_conv_b[l], ffn_w_down[l])
    return x


def setup_inputs(seed: int = 0) -> dict:
    key = jax.random.key(seed)
    ks = jax.random.split(key, 24)
    f32 = jnp.float32

    def nrm(k, shape, s):
        return jax.random.normal(k, shape, f32) * s

    L, D = DEPTH, D_MODEL
    dt = jnp.exp(jax.random.uniform(ks[12], (L, N_DIR, N_DN_HEADS), f32, np.log(1e-3), np.log(1e-1)))
    return {
        'x_prompt': nrm(ks[0], (BATCH, SEQ, D), 1.0),
        'x_sample': nrm(ks[1], (DEC_BATCH, DEC_SEQ, D), 1.0),
        'c_prompt': nrm(ks[2], (BATCH, D), 1.0),
        'c_sample': nrm(ks[3], (DEC_BATCH, D), 1.0),
        'ada_w': nrm(ks[4], (L, D, 6 * D), D ** -0.5),
        'ada_b': nrm(ks[5], (L, 6 * D), 0.02),
        'norm1_g': 1.0 + nrm(ks[6], (L, D), 0.02),
        'norm2_g': 1.0 + nrm(ks[7], (L, D), 0.02),
        'w_in': nrm(ks[8], (L, D, IN_COLS), D ** -0.5),
        'att_q_norm': 1.0 + nrm(ks[9], (L, ATT_HD), 0.02),
        'att_k_norm': 1.0 + nrm(ks[10], (L, ATT_HD), 0.02),
        'att_rpb': nrm(ks[11], (L, N_ATT_HEADS, 2 * WIN_R - 1, 2 * WIN_C - 1), 0.02),
        'dn_conv_w': nrm(ks[13], (L, DN_CONV, 3 * DN_W), DN_CONV ** -0.5),
        'dn_a_log': jnp.log(jax.random.uniform(ks[14], (L, N_DIR, N_DN_HEADS), f32, 1.0, 16.0)),
        'dn_dt_bias': dt + jnp.log(-jnp.expm1(-dt)),
        'dn_out_norm': 1.0 + nrm(ks[15], (L, DN_HD), 0.02),
        'w_o': nrm(ks[16], (L, D, D), D ** -0.5),
        'ffn_w_up': nrm(ks[17], (L, D, 2 * D_FF), D ** -0.5),
        'ffn_conv_w': nrm(ks[18], (L, FFN_CONV, 2 * D_FF), FFN_CONV ** -0.5),
        'ffn_conv_b': nrm(ks[19], (L, 2 * D_FF), 0.02),
        'ffn_w_down': nrm(ks[20], (L, D_FF, D), D_FF ** -0.5),
    }


def reference(x_prompt, x_sample, c_prompt, c_sample, ada_w, ada_b, norm1_g, norm2_g, w_in,
              att_q_norm, att_k_norm, att_rpb, dn_conv_w, dn_a_log, dn_dt_bias, dn_out_norm, w_o,
              ffn_w_up, ffn_conv_w, ffn_conv_b, ffn_w_down):
    y_prompt = _trunk(x_prompt, c_prompt, ada_w, ada_b, norm1_g, norm2_g, w_in, att_q_norm, att_k_norm,
                      att_rpb, dn_conv_w, dn_a_log, dn_dt_bias, dn_out_norm, w_o,
                      ffn_w_up, ffn_conv_w, ffn_conv_b, ffn_w_down)
    y_sample = _trunk(x_sample, c_sample, ada_w, ada_b, norm1_g, norm2_g, w_in, att_q_norm, att_k_norm,
                      att_rpb, dn_conv_w, dn_a_log, dn_dt_bias, dn_out_norm, w_o,
                      ffn_w_up, ffn_conv_w, ffn_conv_b, ffn_w_down)
    return (y_prompt, y_sample)
```

```python
import functools

import numpy as np
import jax
import jax.numpy as jnp
from jax import lax
from jax.experimental import pallas as pl
from jax.experimental.pallas import tpu as pltpu

F32 = jnp.float32
BF16 = jnp.bfloat16

D_MODEL = 1024
GRID_W = 64
N_ATT_HEADS = 8
ATT_HD = 64
ATT_W = N_ATT_HEADS * ATT_HD
WIN_R = 8
WIN_C = 16
N_DN_HEADS = 4
DN_HD = 128
DN_W = N_DN_HEADS * DN_HD
N_DIR = 2
CHUNK = 64
D_FF = 2816
NORM_EPS = 1e-6
N_GATE = 2 * N_DIR * N_DN_HEADS
LANES = 128
GATE_PAD = LANES
IN_COLS_PAD = 3 * ATT_W + 4 * DN_W + GATE_PAD
MASK_NEG = -1e30

VMEM_LIMIT = 56 * 1024 * 1024
TOKEN_TILE = 512
FFN_CHUNK = 256
HALO = 16


def _cparams(n_axes):
    return pltpu.CompilerParams(dimension_semantics=("arbitrary",) * n_axes,
                                vmem_limit_bytes=VMEM_LIMIT)


def _const_spec(shape):
    nd = len(shape)
    return pl.BlockSpec(shape, lambda *_: (0,) * nd, pipeline_mode=pl.Buffered(1))


def _silu(x):
    return x * jax.nn.sigmoid(x)


def _dot(a, b):
    return jnp.dot(a, b, preferred_element_type=F32)


def _dot_nt(a, b):
    return lax.dot_general(a, b, (((1,), (1,)), ((), ())), preferred_element_type=F32)


def _dot_tn(a, b):
    return lax.dot_general(a, b, (((0,), (0,)), ((), ())), preferred_element_type=F32)


def _split2(x):
    hi = x.astype(BF16)
    lo = (x - hi.astype(F32)).astype(BF16)
    return hi, lo


def _split3(x):
    hi = x.astype(BF16)
    r = x - hi.astype(F32)
    mid = r.astype(BF16)
    lo = (r - mid.astype(F32)).astype(BF16)
    return hi, mid, lo


def _mod_kernel(c_ref, w_ref, b_ref, o_ref):
    s = _silu(c_ref[...])
    o_ref[...] = jnp.dot(s, w_ref[...], preferred_element_type=F32,
                         precision=lax.Precision.HIGHEST) + b_ref[...]


def _mod_call(c, ada_w, ada_b):
    bp, d = c.shape
    n = ada_w.shape[1]
    bn = 1024
    return pl.pallas_call(
        _mod_kernel,
        grid=(n // bn,),
        in_specs=[pl.BlockSpec((bp, d), lambda j: (0, 0)),
                  pl.BlockSpec((d, bn), lambda j: (0, j)),
                  pl.BlockSpec((1, bn), lambda j: (0, j))],
        out_specs=pl.BlockSpec((bp, bn), lambda j: (0, j)),
        out_shape=jax.ShapeDtypeStruct((bp, n), F32),
        compiler_params=_cparams(1),
        name="mod",
    )(c, ada_w, ada_b.reshape(1, n))


def _inproj_kernel(x_ref, mod_ref, g_ref, w_ref, alog_ref, dtb_ref,
                   att_ref, dn_ref, z_ref, gate_ref):
    x = x_ref[...]
    ms = jnp.mean(x * x, axis=-1, keepdims=True)
    y = x * lax.rsqrt(ms + NORM_EPS) * g_ref[...]
    h = (y * (1.0 + mod_ref[0, 1:2, :]) + mod_ref[0, 0:1, :]).astype(BF16)
    cw = 512
    for j in range(3 * ATT_W // cw):
        att_ref[:, j * cw:(j + 1) * cw] = _dot(h, w_ref[:, j * cw:(j + 1) * cw]).astype(BF16)
    o = 3 * ATT_W
    for j in range(3 * DN_W // cw):
        dn_ref[:, j * cw:(j + 1) * cw] = _dot(h, w_ref[:, o + j * cw:o + (j + 1) * cw])
    o += 3 * DN_W
    z_ref[...] = _dot(h, w_ref[:, o:o + DN_W]).astype(BF16)
    o += DN_W
    graw = _dot(h, w_ref[:, o:o + GATE_PAD])
    lane = lax.broadcasted_iota(jnp.int32, (1, GATE_PAD), 1)
    t = graw + dtb_ref[...]
    softplus = jnp.maximum(t, 0.0) + jnp.log1p(jnp.exp(-jnp.abs(t)))
    decay = -jnp.exp(alog_ref[...]) * softplus
    gate_ref[...] = jnp.where(lane < N_DIR * N_DN_HEADS, jax.nn.sigmoid(graw), decay)


def _inproj_call(x2, mod3, g1, w_cat, alog_l, dtb_l, seq):
    n_tok = x2.shape[0]
    tm = TOKEN_TILE
    per = seq // tm
    tok = lambda i: (i, 0)
    return pl.pallas_call(
        _inproj_kernel,
        grid=(n_tok // tm,),
        in_specs=[pl.BlockSpec((tm, D_MODEL), tok),
                  pl.BlockSpec((1, 6, D_MODEL), lambda i: (i // per, 0, 0)),
                  _const_spec((1, D_MODEL)),
                  _const_spec((D_MODEL, IN_COLS_PAD)),
                  _const_spec((1, GATE_PAD)),
                  _const_spec((1, GATE_PAD))],
        out_specs=[pl.BlockSpec((tm, 3 * ATT_W), tok),
                   pl.BlockSpec((tm, 3 * DN_W), tok),
                   pl.BlockSpec((tm, DN_W), tok),
                   pl.BlockSpec((tm, GATE_PAD), tok)],
        out_shape=[jax.ShapeDtypeStruct((n_tok, 3 * ATT_W), BF16),
                   jax.ShapeDtypeStruct((n_tok, 3 * DN_W), F32),
                   jax.ShapeDtypeStruct((n_tok, DN_W), BF16),
                   jax.ShapeDtypeStruct((n_tok, GATE_PAD), F32)],
        compiler_params=_cparams(1),
        name="inproj",
    )(x2, mod3, g1, w_cat, alog_l, dtb_l)


def _natten_kernel(q_ref, k_ref, v_ref, bias_ref, qg_ref, kg_ref, ones_ref, o_ref,
                   qa_s, qb_s, kn_s, *, rows):
    lane = lax.broadcasted_iota(jnp.int32, (1, LANES), 1)
    first = lane < ATT_HD

    def head_norm(x, g):
        ss = _dot((x * x).astype(BF16), ones_ref[...])
        return x * lax.rsqrt(ss * (1.0 / ATT_HD) + NORM_EPS) * g

    def prep(r, carry):
        rs = pl.ds(pl.multiple_of(r * GRID_W, GRID_W), GRID_W)
        q = head_norm(q_ref[0, rs, :].astype(F32), qg_ref[...]) * (ATT_HD ** -0.5)
        qa_s[rs, :] = jnp.where(first, q, 0.0).astype(BF16)
        qb_s[rs, :] = jnp.where(first, 0.0, q).astype(BF16)
        kn_s[rs, :] = head_norm(k_ref[0, rs, :].astype(F32), kg_ref[...]).astype(BF16)
        return carry

    lax.fori_loop(0, rows, prep, 0)

    def row(r, carry):
        r0 = jnp.clip(r - WIN_R // 2, 0, rows - WIN_R)
        dd = r - r0
        qs = pl.ds(pl.multiple_of(r * GRID_W, GRID_W), GRID_W)
        ks = pl.ds(pl.multiple_of(r0 * GRID_W, GRID_W), WIN_R * GRID_W)
        kb = kn_s[ks, :]
        vb = v_ref[0, ks, :]
        outs = []
        for hh, q_s in enumerate((qa_s, qb_s)):
            s = _dot_nt(q_s[qs, :], kb) + bias_ref[hh, dd]
            m = jnp.max(s, axis=-1, keepdims=True)
            p = jnp.exp(s - m)
            l = jnp.sum(p, axis=-1, keepdims=True)
            outs.append(_dot(p.astype(BF16), vb) / l)
        o_ref[0, qs, :] = jnp.where(first, outs[0], outs[1]).astype(BF16)
        return carry

    lax.fori_loop(0, rows, row, 0)


def _natten_call(att, bias_tbl, qg2, kg2, ones_bd, batch, seq):
    rows = seq // GRID_W
    assert rows >= WIN_R
    att3 = att.reshape(batch, seq, 3 * ATT_W)
    n_pair = N_ATT_HEADS // 2
    blk = (1, seq, LANES)
    return pl.pallas_call(
        functools.partial(_natten_kernel, rows=rows),
        grid=(n_pair, batch),
        in_specs=[pl.BlockSpec(blk, lambda p, b: (b, 0, p)),
                  pl.BlockSpec(blk, lambda p, b: (b, 0, n_pair + p)),
                  pl.BlockSpec(blk, lambda p, b: (b, 0, 2 * n_pair + p)),
                  pl.BlockSpec((2, WIN_R, GRID_W, WIN_R * GRID_W), lambda p, b: (p, 0, 0, 0)),
                  _const_spec((1, LANES)),
                  _const_spec((1, LANES)),
                  _const_spec((LANES, LANES))],
        out_specs=pl.BlockSpec(blk, lambda p, b: (b, 0, p)),
        out_shape=jax.ShapeDtypeStruct((batch, seq, ATT_W), BF16),
        scratch_shapes=[pltpu.VMEM((seq, LANES), BF16)] * 3,
        compiler_params=_cparams(2),
        name="natten",
    )(att3, att3, att3, bias_tbl, qg2, kg2, ones_bd)


def _natten_bias_table(rpb):
    dd = np.arange(WIN_R)[:, None, None, None]
    qc = np.arange(GRID_W)[None, :, None, None]
    bi = np.arange(WIN_R)[None, None, :, None]
    kc = np.arange(GRID_W)[None, None, None, :]
    ridx = np.broadcast_to(bi - dd + WIN_R - 1, (WIN_R, GRID_W, WIN_R, GRID_W))
    cidx = np.broadcast_to(np.clip(kc - qc + WIN_C - 1, 0, 2 * WIN_C - 2), ridx.shape)
    c0 = np.clip(qc - WIN_C // 2, 0, GRID_W - WIN_C)
    valid = np.broadcast_to((kc >= c0) & (kc < c0 + WIN_C), ridx.shape)
    tbl = rpb[:, ridx, cidx]
    tbl = jnp.where(valid[None], tbl, MASK_NEG)
    return tbl.reshape(N_ATT_HEADS, WIN_R, GRID_W, WIN_R * GRID_W)


def _gdn_kernel(q_ref, k_ref, v_ref, z_ref, gate_ref, wq_ref, wk_ref, wv_ref, gn_ref, o_ref,
                qpad, kpad, vpad, u_s, wq_s, kd_s, in_s, eg_s, o_s, *, seq):
    n_chunk = seq // CHUNK
    head = pl.program_id(1)
    C = CHUNK
    lane = lax.broadcasted_iota(jnp.int32, (1, LANES), 1)
    left = lane < C
    ri = lax.broadcasted_iota(jnp.int32, (C, LANES), 0)
    li = lax.broadcasted_iota(jnp.int32, (C, LANES), 1)
    ci = li & (C - 1)
    fwd_half = li < C
    lag = jnp.where(fwd_half, ri - ci, ci - ri)
    causal_p = lag >= 0
    strict_p = lag > 0
    eye_p = (lag == 0).astype(F32)
    eye_left = (li == ri)
    b16 = ((ci >> 4) == (ri >> 4)).astype(F32)
    b32 = ((ci >> 5) == (ri >> 5)).astype(F32)
    m16 = b16
    m32 = b32 - b16
    m64 = 1.0 - b32

    zpad = jnp.zeros((8, LANES), F32)
    for src, dst in ((q_ref, qpad), (k_ref, kpad), (v_ref, vpad)):
        dst[0:8, :] = zpad
        dst[seq + 8:seq + 16, :] = zpad
        dst[8:seq + 8, :] = src[0]
    o_s[...] = jnp.zeros_like(o_s)

    def blockdiag(p16):
        return jnp.concatenate([jnp.where(left, p16, 0), jnp.where(left, 0, p16)], axis=0)

    def pmm(a, b):
        ah, al = _split2(a)
        bh, bl = _split2(b)
        bdh = blockdiag(bh)
        r = _dot(jnp.concatenate([ah, al], axis=0), bdh)
        return r[:C] + r[C:] + _dot(ah, blockdiag(bl))

    def lane_pick(x, col):
        return jnp.sum(jnp.where(lane == col, x, 0.0), axis=-1, keepdims=True)

    def phase_a(c, carry):
        base = pl.multiple_of(c * C, C)

        def conv_silu(pad_ref, w_ref):
            blk = pad_ref[pl.ds(base, C + 16), :]
            y = (blk[7:7 + C] * w_ref[0:1, :] + blk[8:8 + C] * w_ref[1:2, :]
                 + blk[9:9 + C] * w_ref[2:3, :])
            return _silu(y)

        q = conv_silu(qpad, wq_ref)
        k = conv_silu(kpad, wk_ref)
        v = conv_silu(vpad, wv_ref)
        q = q * lax.rsqrt(jnp.sum(q * q, axis=-1, keepdims=True) + NORM_EPS) * (DN_HD ** -0.5)
        k = k * lax.rsqrt(jnp.sum(k * k, axis=-1, keepdims=True) + NORM_EPS)
        k16 = k.astype(BF16)
        gram = _dot_nt(jnp.concatenate([k16, q.astype(BF16)], axis=0), k16)
        gkk = jnp.concatenate([gram[:C], gram[:C]], axis=1)
        gqk = jnp.concatenate([gram[C:], gram[C:]], axis=1)

        gat = gate_ref[0, pl.ds(base, C), :]
        beta, gcol, grow, glast = [], [], [], []
        for d in range(N_DIR):
            beta.append(lane_pick(gat, d * N_DN_HEADS + head))
            g = lane_pick(gat, N_DIR * N_DN_HEADS + d * N_DN_HEADS + head)
            gb = jnp.broadcast_to(g, (C, LANES))
            inc = (ri <= ci) if d == 0 else (ri >= ci)
            gr = jnp.sum(jnp.where(inc, gb, 0.0), axis=0, keepdims=True)
            grow.append(gr)
            gcol.append(jnp.sum(jnp.where(eye_left, jnp.broadcast_to(gr, (C, LANES)), 0.0),
                                axis=-1, keepdims=True))
            glast.append(jnp.sum(g, axis=0, keepdims=True))
        beta_p = jnp.where(fwd_half, beta[0], beta[1])
        diff = jnp.where(fwd_half, gcol[0] - grow[0], gcol[1] - grow[1])
        decay = jnp.exp(jnp.where(causal_p, diff, MASK_NEG))
        l_p = jnp.where(strict_p, beta_p * gkk * decay, 0.0)
        in_s[pl.ds(base, C), :] = jnp.where(causal_p, gqk * decay, 0.0).astype(BF16)

        x1 = -l_p * m16
        x2 = pmm(x1, x1)
        t = eye_p + x1
        t = t + pmm(t, x2)
        x4 = pmm(x2, x2)
        t = t + pmm(t, x4)
        x8 = pmm(x4, x4)
        t = t + pmm(t, x8)
        t = t - pmm(pmm(t, l_p * m32), t)
        t = t - pmm(pmm(t, l_p * m64), t)

        rhs = []
        for d in range(N_DIR):
            eg = jnp.exp(gcol[d])
            kb = k * beta[d]
            rhs.append(jnp.concatenate([v * beta[d], kb * eg], axis=1))
            kd_s[d, pl.ds(base, C), :] = (k * jnp.exp(glast[d] - gcol[d])).astype(BF16)
            wq_s[d, pl.ds(pl.multiple_of(c * 2 * C, 2 * C) + C, C), :] = (q * eg).astype(BF16)
            eg_s[d, pl.ds(pl.multiple_of(c * 8, 8), 8), :] = jnp.broadcast_to(
                jnp.exp(glast[d]), (8, LANES))
        rhs = jnp.concatenate(rhs, axis=0)
        th, tl = _split2(t)
        rh, rl = _split2(rhs)
        bdt = blockdiag(th)
        uw = _dot(bdt, rh) + _dot(bdt, rl) + _dot(blockdiag(tl), rh)
        for d in range(N_DIR):
            u_s[d, pl.ds(base, C), :] = uw[d * C:(d + 1) * C, :LANES]
            wq_s[d, pl.ds(pl.multiple_of(c * 2 * C, 2 * C), C), :] = (
                uw[d * C:(d + 1) * C, LANES:].astype(BF16))
        return carry

    lax.fori_loop(0, n_chunk, phase_a, 0)

    def phase_b(n, states):
        new_states = []
        for d in range(N_DIR):
            c = n if d == 0 else n_chunk - 1 - n
            base = pl.multiple_of(c * C, C)
            s = states[d]
            r = _dot(wq_s[d, pl.ds(pl.multiple_of(c * 2 * C, 2 * C), 2 * C), :], s.astype(BF16))
            vn = (u_s[d, pl.ds(base, C), :] - r[:C]).astype(BF16)
            intra = in_s[pl.ds(base, C), :]
            intra = jnp.where(left, intra, 0) if d == 0 else jnp.where(left, 0, intra)
            o = r[C:] + _dot(intra, jnp.concatenate([vn, vn], axis=0))
            o_s[pl.ds(base, C), :] += o
            eg = eg_s[d, pl.ds(pl.multiple_of(c * 8, 8), 1), :]
            new_states.append(s * eg + _dot_tn(kd_s[d, pl.ds(base, C), :], vn))
        return tuple(new_states)

    zero = jnp.zeros((DN_HD, DN_HD), F32)
    lax.fori_loop(0, n_chunk, phase_b, (zero, zero))

    def finish(c, carry):
        rs = pl.ds(pl.multiple_of(c * C, C), C)
        o = o_s[rs, :]
        o = o * lax.rsqrt(jnp.mean(o * o, axis=-1, keepdims=True) + NORM_EPS) * gn_ref[...]
        o_ref[0, rs, :] = (o * _silu(z_ref[0, rs, :].astype(F32))).astype(BF16)
        return carry

    lax.fori_loop(0, n_chunk, finish, 0)


def _gdn_call(dn, z, gates, conv_w, out_norm, batch, seq):
    dn3 = dn.reshape(batch, seq, 3 * DN_W)
    z3 = z.reshape(batch, seq, DN_W)
    g3 = gates.reshape(batch, seq, GATE_PAD)
    blk = (1, seq, LANES)
    nh = N_DN_HEADS
    n_chunk = seq // CHUNK
    return pl.pallas_call(
        functools.partial(_gdn_kernel, seq=seq),
        grid=(batch, nh),
        in_specs=[pl.BlockSpec(blk, lambda b, h: (b, 0, h)),
                  pl.BlockSpec(blk, lambda b, h: (b, 0, nh + h)),
                  pl.BlockSpec(blk, lambda b, h: (b, 0, 2 * nh + h)),
                  pl.BlockSpec(blk, lambda b, h: (b, 0, h)),
                  pl.BlockSpec(blk, lambda b, h: (b, 0, 0)),
                  pl.BlockSpec((3, LANES), lambda b, h: (0, h)),
                  pl.BlockSpec((3, LANES), lambda b, h: (0, nh + h)),
                  pl.BlockSpec((3, LANES), lambda b, h: (0, 2 * nh + h)),
                  _const_spec((1, LANES))],
        out_specs=pl.BlockSpec(blk, lambda b, h: (b, 0, h)),
        out_shape=jax.ShapeDtypeStruct((batch, seq, DN_W), BF16),
        scratch_shapes=[pltpu.VMEM((seq + 16, LANES), F32)] * 3 + [
            pltpu.VMEM((N_DIR, seq, LANES), F32),
            pltpu.VMEM((N_DIR, 2 * seq, LANES), BF16),
            pltpu.VMEM((N_DIR, seq, LANES), BF16),
            pltpu.VMEM((seq, LANES), BF16),
            pltpu.VMEM((N_DIR, 8 * n_chunk, LANES), F32),
            pltpu.VMEM((seq, LANES), F32)],
        compiler_params=_cparams(2),
        name="gdn",
    )(dn3, dn3, dn3, z3, g3, conv_w, conv_w, conv_w, out_norm)


def _oproj_kernel(att_ref, dn_ref, x_ref, mod_ref, g_ref, w_ref, x1_ref, h_ref):
    mix = _dot(att_ref[...], w_ref[0:ATT_W, :]) + _dot(dn_ref[...], w_ref[ATT_W:, :])
    x1 = x_ref[...] + mod_ref[0, 2:3, :] * mix
    x1_ref[...] = x1
    ms = jnp.mean(x1 * x1, axis=-1, keepdims=True)
    y = x1 * lax.rsqrt(ms + NORM_EPS) * g_ref[...]
    h_ref[...] = (y * (1.0 + mod_ref[0, 4:5, :]) + mod_ref[0, 3:4, :]).astype(BF16)


def _oproj_call(att, dno, x2, mod3, g2, w_o, seq):
    n_tok = x2.shape[0]
    tm = TOKEN_TILE
    per = seq // tm
    tok = lambda i: (i, 0)
    return pl.pallas_call(
        _oproj_kernel,
        grid=(n_tok // tm,),
        in_specs=[pl.BlockSpec((tm, ATT_W), tok),
                  pl.BlockSpec((tm, DN_W), tok),
                  pl.BlockSpec((tm, D_MODEL), tok),
                  pl.BlockSpec((1, 6, D_MODEL), lambda i: (i // per, 0, 0)),
                  _const_spec((1, D_MODEL)),
                  _const_spec((D_MODEL, D_MODEL))],
        out_specs=[pl.BlockSpec((tm, D_MODEL), tok),
                   pl.BlockSpec((tm, D_MODEL), tok)],
        out_shape=[jax.ShapeDtypeStruct((n_tok, D_MODEL), F32),
                   jax.ShapeDtypeStruct((n_tok, D_MODEL), BF16)],
        compiler_params=_cparams(1),
        name="oproj",
    )(att, dno, x2, mod3, g2, w_o)


def _ffn_kernel(h_ref, hp_ref, hn_ref, x1_ref, mod_ref, wu_ref, cw_ref, cb_ref, wd_ref, y_ref,
                *, per):
    i = pl.program_id(0)
    tm = h_ref.shape[0]
    zero = jnp.zeros((HALO, D_MODEL), BF16)
    hp = jnp.where(i % per != 0, hp_ref[...], zero)
    hn = jnp.where(i % per != per - 1, hn_ref[...], zero)
    hx = jnp.concatenate([hp, h_ref[...], hn], axis=0)
    acc = jnp.zeros((tm, D_MODEL), F32)
    for j in range(D_FF // FFN_CHUNK):
        halves = []
        for o in (j * FFN_CHUNK, D_FF + j * FFN_CHUNK):
            u = _dot(hx, wu_ref[:, o:o + FFN_CHUNK])
            cw = cw_ref[:, o:o + FFN_CHUNK]
            halves.append(u[HALO - 1:HALO - 1 + tm] * cw[0:1, :] + u[HALO:HALO + tm] * cw[1:2, :]
                          + u[HALO + 1:HALO + 1 + tm] * cw[2:3, :] + cb_ref[:, o:o + FFN_CHUNK])
        act = (_silu(halves[0]) * halves[1]).astype(BF16)
        acc = acc + _dot(act, wd_ref[j * FFN_CHUNK:(j + 1) * FFN_CHUNK, :])
    y_ref[...] = x1_ref[...] + mod_ref[0, 5:6, :] * acc


def _ffn_call(h2, x1, mod3, w_up, conv_w, conv_b, w_down, seq):
    n_tok = x1.shape[0]
    tm = TOKEN_TILE
    per = seq // tm
    hb = tm // HALO
    last = n_tok // HALO - 1
    tok = lambda i: (i, 0)
    return pl.pallas_call(
        functools.partial(_ffn_kernel, per=per),
        grid=(n_tok // tm,),
        in_specs=[pl.BlockSpec((tm, D_MODEL), tok),
                  pl.BlockSpec((HALO, D_MODEL), lambda i: (jnp.maximum(i * hb - 1, 0), 0)),
                  pl.BlockSpec((HALO, D_MODEL), lambda i: (jnp.minimum((i + 1) * hb, last), 0)),
                  pl.BlockSpec((tm, D_MODEL), tok),
                  pl.BlockSpec((1, 6, D_MODEL), lambda i: (i // per, 0, 0)),
                  _const_spec((D_MODEL, 2 * D_FF)),
                  _const_spec((3, 2 * D_FF)),
                  _const_spec((1, 2 * D_FF)),
                  _const_spec((D_FF, D_MODEL))],
        out_specs=pl.BlockSpec((tm, D_MODEL), tok),
        out_shape=jax.ShapeDtypeStruct((n_tok, D_MODEL), F32),
        compiler_params=_cparams(1),
        name="ffn",
    )(h2, h2, h2, x1, mod3, w_up, conv_w, conv_b, w_down)


def _trunk(x, mod, p):
    batch, seq, d = x.shape
    assert seq % TOKEN_TILE == 0 and seq % GRID_W == 0
    x2 = x.reshape(batch * seq, d)
    mod3 = mod.reshape(batch, 6, d)
    att, dn, z, gates = _inproj_call(x2, mod3, p["norm1_g"], p["w_in"], p["alog_l"], p["dtb_l"], seq)
    att_o = _natten_call(att, p["bias_tbl"], p["qg2"], p["kg2"], p["ones_bd"], batch, seq)
    dn_o = _gdn_call(dn, z, gates, p["dn_conv_w"], p["dn_out_norm"], batch, seq)
    x1, h2 = _oproj_call(att_o.reshape(batch * seq, ATT_W), dn_o.reshape(batch * seq, DN_W),
                         x2, mod3, p["norm2_g"], p["w_o"], seq)
    y = _ffn_call(h2, x1, mod3, p["ffn_w_up"], p["ffn_conv_w"], p["ffn_conv_b"], p["ffn_w_down"], seq)
    return y.reshape(batch, seq, d)


def kernel(x_prompt, x_sample, c_prompt, c_sample, ada_w, ada_b, norm1_g, norm2_g, w_in,
           att_q_norm, att_k_norm, att_rpb, dn_conv_w, dn_a_log, dn_dt_bias, dn_out_norm, w_o,
           ffn_w_up, ffn_conv_w, ffn_conv_b, ffn_w_down):
    depth = ada_w.shape[0]
    bp, bs = c_prompt.shape[0], c_sample.shape[0]
    n_c = -(-(bp + bs) // 8) * 8
    gate_lanes = lambda a: jnp.pad(a.reshape(1, N_DIR * N_DN_HEADS),
                                   ((0, 0), (N_DIR * N_DN_HEADS, GATE_PAD - N_GATE)))
    blk = np.arange(LANES) // ATT_HD
    ones_bd = jnp.asarray(blk[:, None] == blk[None, :], BF16)
    xs = [x_prompt, x_sample]
    for l in range(depth):
        p = {
            "norm1_g": norm1_g[l].reshape(1, -1), "norm2_g": norm2_g[l].reshape(1, -1),
            "w_in": jnp.pad(w_in[l], ((0, 0), (0, GATE_PAD - N_GATE))).astype(BF16),
            "alog_l": gate_lanes(dn_a_log[l]), "dtb_l": gate_lanes(dn_dt_bias[l]),
            "bias_tbl": _natten_bias_table(att_rpb[l]),
            "qg2": jnp.tile(att_q_norm[l], 2).reshape(1, LANES),
            "kg2": jnp.tile(att_k_norm[l], 2).reshape(1, LANES),
            "ones_bd": ones_bd,
            "dn_conv_w": dn_conv_w[l], "dn_out_norm": dn_out_norm[l].reshape(1, -1),
            "w_o": w_o[l].astype(BF16), "ffn_w_up": ffn_w_up[l].astype(BF16),
            "ffn_conv_w": ffn_conv_w[l], "ffn_conv_b": ffn_conv_b[l].reshape(1, -1),
            "ffn_w_down": ffn_w_down[l].astype(BF16),
        }
        c_all = jnp.pad(jnp.concatenate([c_prompt, c_sample], axis=0), ((0, n_c - bp - bs), (0, 0)))
        mod = _mod_call(c_all, ada_w[l], ada_b[l])
        xs = [_trunk(xs[0], mod[:bp], p), _trunk(xs[1], mod[bp:bp + bs], p)]
    return tuple(xs)
```

```python
import functools

import numpy as np
import jax
import jax.numpy as jnp
from jax import lax
from jax.experimental import pallas as pl
from jax.experimental.pallas import tpu as pltpu

F32 = jnp.float32
BF16 = jnp.bfloat16

D_MODEL = 1024
GRID_W = 64
N_ATT_HEADS = 8
ATT_HD = 64
ATT_W = N_ATT_HEADS * ATT_HD
WIN_R = 8
WIN_C = 16
N_DN_HEADS = 4
DN_HD = 128
DN_W = N_DN_HEADS * DN_HD
N_DIR = 2
CHUNK = 64
D_FF = 2816
NORM_EPS = 1e-6
N_GATE = 2 * N_DIR * N_DN_HEADS
LANES = 128
GATE_PAD = LANES
IN_COLS_PAD = 3 * ATT_W + 4 * DN_W + GATE_PAD
MASK_NEG = -1e30

VMEM_LIMIT = 56 * 1024 * 1024
TOKEN_TILE = 512
FFN_CHUNK = 256
HALO = 16
AQ_ROWS = DN_HD + CHUNK
NATTEN_UNROLL = 8
GDN_UNROLL = 8


def _cparams(n_axes):
    return pltpu.CompilerParams(dimension_semantics=("arbitrary",) * n_axes,
                                vmem_limit_bytes=VMEM_LIMIT)


def _const_spec(shape):
    nd = len(shape)
    return pl.BlockSpec(shape, lambda *_: (0,) * nd, pipeline_mode=pl.Buffered(1))


def _silu(x):
    return x * jax.nn.sigmoid(x)


def _dot(a, b):
    return jnp.dot(a, b, preferred_element_type=F32)


def _dot_nt(a, b):
    return lax.dot_general(a, b, (((1,), (1,)), ((), ())), preferred_element_type=F32)


def _dot_tn(a, b):
    return lax.dot_general(a, b, (((0,), (0,)), ((), ())), preferred_element_type=F32)


def _split2(x):
    hi = x.astype(BF16)
    lo = (x - hi.astype(F32)).astype(BF16)
    return hi, lo


def _split3(x):
    hi = x.astype(BF16)
    r = x - hi.astype(F32)
    mid = r.astype(BF16)
    lo = (r - mid.astype(F32)).astype(BF16)
    return hi, mid, lo


def _mod_kernel(c_ref, w_ref, b_ref, o_ref):
    s = _silu(c_ref[...])
    o_ref[...] = jnp.dot(s, w_ref[...], preferred_element_type=F32,
                         precision=lax.Precision.HIGHEST) + b_ref[...]


def _mod_call(c, ada_w, ada_b):
    bp, d = c.shape
    n = ada_w.shape[1]
    bn = 1024
    return pl.pallas_call(
        _mod_kernel,
        grid=(n // bn,),
        in_specs=[pl.BlockSpec((bp, d), lambda j: (0, 0)),
                  pl.BlockSpec((d, bn), lambda j: (0, j)),
                  pl.BlockSpec((1, bn), lambda j: (0, j))],
        out_specs=pl.BlockSpec((bp, bn), lambda j: (0, j)),
        out_shape=jax.ShapeDtypeStruct((bp, n), F32),
        compiler_params=_cparams(1),
        name="mod",
    )(c, ada_w, ada_b.reshape(1, n))


def _inproj_kernel(x_ref, mod_ref, g_ref, w_ref, alog_ref, dtb_ref,
                   att_ref, dn_ref, z_ref, gate_ref):
    x = x_ref[...]
    ms = jnp.mean(x * x, axis=-1, keepdims=True)
    y = x * lax.rsqrt(ms + NORM_EPS) * g_ref[...]
    h = (y * (1.0 + mod_ref[0, 1:2, :]) + mod_ref[0, 0:1, :]).astype(BF16)
    cw = 512
    for j in range(3 * ATT_W // cw):
        att_ref[:, j * cw:(j + 1) * cw] = _dot(h, w_ref[:, j * cw:(j + 1) * cw]).astype(BF16)
    o = 3 * ATT_W
    for j in range(3 * DN_W // cw):
        dn_ref[:, j * cw:(j + 1) * cw] = _dot(h, w_ref[:, o + j * cw:o + (j + 1) * cw])
    o += 3 * DN_W
    z_ref[...] = _dot(h, w_ref[:, o:o + DN_W]).astype(BF16)
    o += DN_W
    graw = _dot(h, w_ref[:, o:o + GATE_PAD])
    lane = lax.broadcasted_iota(jnp.int32, (1, GATE_PAD), 1)
    t = graw + dtb_ref[...]
    softplus = jnp.maximum(t, 0.0) + jnp.log1p(jnp.exp(-jnp.abs(t)))
    decay = -jnp.exp(alog_ref[...]) * softplus
    gate_ref[...] = jnp.where(lane < N_DIR * N_DN_HEADS, jax.nn.sigmoid(graw), decay)


def _inproj_call(x2, mod3, g1, w_cat, alog_l, dtb_l, seq):
    n_tok = x2.shape[0]
    tm = TOKEN_TILE
    per = seq // tm
    tok = lambda i: (i, 0)
    return pl.pallas_call(
        _inproj_kernel,
        grid=(n_tok // tm,),
        in_specs=[pl.BlockSpec((tm, D_MODEL), tok),
                  pl.BlockSpec((1, 6, D_MODEL), lambda i: (i // per, 0, 0)),
                  _const_spec((1, D_MODEL)),
                  _const_spec((D_MODEL, IN_COLS_PAD)),
                  _const_spec((1, GATE_PAD)),
                  _const_spec((1, GATE_PAD))],
        out_specs=[pl.BlockSpec((tm, 3 * ATT_W), tok),
                   pl.BlockSpec((tm, 3 * DN_W), tok),
                   pl.BlockSpec((tm, DN_W), tok),
                   pl.BlockSpec((tm, GATE_PAD), tok)],
        out_shape=[jax.ShapeDtypeStruct((n_tok, 3 * ATT_W), BF16),
                   jax.ShapeDtypeStruct((n_tok, 3 * DN_W), F32),
                   jax.ShapeDtypeStruct((n_tok, DN_W), BF16),
                   jax.ShapeDtypeStruct((n_tok, GATE_PAD), F32)],
        compiler_params=_cparams(1),
        name="inproj",
    )(x2, mod3, g1, w_cat, alog_l, dtb_l)


def _natten_kernel(q_ref, k_ref, v_ref, bias_ref, qg_ref, kg_ref, ones_ref, o_ref,
                   qa_s, qb_s, kn_s, *, rows):
    lane = lax.broadcasted_iota(jnp.int32, (1, LANES), 1)
    first = lane < ATT_HD

    def head_norm(x, g):
        ss = _dot((x * x).astype(BF16), ones_ref[...])
        return x * lax.rsqrt(ss * (1.0 / ATT_HD) + NORM_EPS) * g

    def prep(r, carry):
        rs = pl.ds(pl.multiple_of(r * GRID_W, GRID_W), GRID_W)
        q = head_norm(q_ref[0, rs, :].astype(F32), qg_ref[...]) * (ATT_HD ** -0.5)
        qa_s[rs, :] = jnp.where(first, q, 0.0).astype(BF16)
        qb_s[rs, :] = jnp.where(first, 0.0, q).astype(BF16)
        kn_s[rs, :] = head_norm(k_ref[0, rs, :].astype(F32), kg_ref[...]).astype(BF16)
        return carry

    lax.fori_loop(0, rows, prep, 0, unroll=2)

    def row_group(i, carry):
        units = []
        for j in range(NATTEN_UNROLL):
            r = i * NATTEN_UNROLL + j
            r0 = jnp.clip(r - WIN_R // 2, 0, rows - WIN_R)
            qs = pl.ds(pl.multiple_of(r * GRID_W, GRID_W), GRID_W)
            ks = pl.ds(pl.multiple_of(r0 * GRID_W, GRID_W), WIN_R * GRID_W)
            units.append((qs, ks, r - r0))
        scores = [[_dot_nt(q_s[qs, :], kn_s[ks, :]) + bias_ref[hh, dd]
                   for hh, q_s in enumerate((qa_s, qb_s))] for qs, ks, dd in units]
        probs = []
        for row_scores in scores:
            row_probs = []
            for s in row_scores:
                p = jnp.exp(s - jnp.max(s, axis=-1, keepdims=True))
                row_probs.append((p.astype(BF16), jnp.sum(p, axis=-1, keepdims=True)))
            probs.append(row_probs)
        for (qs, ks, _), row_probs in zip(units, probs):
            vb = v_ref[0, ks, :]
            oa, ob = [_dot(p, vb) / l for p, l in row_probs]
            o_ref[0, qs, :] = jnp.where(first, oa, ob).astype(BF16)
        return carry

    lax.fori_loop(0, rows // NATTEN_UNROLL, row_group, 0)


def _natten_call(att, bias_tbl, qg2, kg2, ones_bd, batch, seq):
    rows = seq // GRID_W
    assert rows >= WIN_R
    att3 = att.reshape(batch, seq, 3 * ATT_W)
    n_pair = N_ATT_HEADS // 2
    blk = (1, seq, LANES)
    return pl.pallas_call(
        functools.partial(_natten_kernel, rows=rows),
        grid=(n_pair, batch),
        in_specs=[pl.BlockSpec(blk, lambda p, b: (b, 0, p)),
                  pl.BlockSpec(blk, lambda p, b: (b, 0, n_pair + p)),
                  pl.BlockSpec(blk, lambda p, b: (b, 0, 2 * n_pair + p)),
                  pl.BlockSpec((2, WIN_R, GRID_W, WIN_R * GRID_W), lambda p, b: (p, 0, 0, 0)),
                  _const_spec((1, LANES)),
                  _const_spec((1, LANES)),
                  _const_spec((LANES, LANES))],
        out_specs=pl.BlockSpec(blk, lambda p, b: (b, 0, p)),
        out_shape=jax.ShapeDtypeStruct((batch, seq, ATT_W), BF16),
        scratch_shapes=[pltpu.VMEM((seq, LANES), BF16)] * 3,
        compiler_params=_cparams(2),
        name="natten",
    )(att3, att3, att3, bias_tbl, qg2, kg2, ones_bd)


def _natten_bias_table(rpb):
    n_row = 2 * WIN_R - 1
    per = 2 * GRID_W
    v = jnp.zeros((N_ATT_HEADS, n_row, per), F32)
    v = v.at[:, :, :WIN_C].set(rpb[:, :, WIN_C - 1:])
    v = v.at[:, :, per - (WIN_C - 1):].set(rpb[:, :, :WIN_C - 1])
    toep = jnp.tile(v, (1, 1, GRID_W))[:, :, :GRID_W * (per - 1)]
    toep = toep.reshape(N_ATT_HEADS, n_row, GRID_W, per - 1)[:, :, :, :GRID_W]
    qc = np.arange(GRID_W)[:, None]
    kc = np.arange(GRID_W)[None, :]
    c0 = np.clip(qc - WIN_C // 2, 0, GRID_W - WIN_C)
    valid = (kc >= c0) & (kc < c0 + WIN_C)
    toep = jnp.where(valid[None, None], toep, MASK_NEG)
    bands = [jnp.swapaxes(toep[:, WIN_R - 1 - dd:2 * WIN_R - 1 - dd], 1, 2) for dd in range(WIN_R)]
    tbl = jnp.stack(bands, axis=1)
    return tbl.reshape(N_ATT_HEADS, WIN_R, GRID_W, WIN_R * GRID_W)


def _gdn_kernel(q_ref, k_ref, v_ref, z_ref, gate_ref, wq_ref, wk_ref, wv_ref, gn_ref, o_ref,
                qpad, kpad, vpad, aq_s, b_s, eg_s, o_s, *, seq):
    n_chunk = seq // CHUNK
    head = pl.program_id(1)
    C = CHUNK
    lane = lax.broadcasted_iota(jnp.int32, (1, LANES), 1)
    left = lane < C
    ri = lax.broadcasted_iota(jnp.int32, (C, LANES), 0)
    li = lax.broadcasted_iota(jnp.int32, (C, LANES), 1)
    ci = li & (C - 1)
    fwd_half = li < C
    lag = jnp.where(fwd_half, ri - ci, ci - ri)
    causal_p = lag >= 0
    strict_p = lag > 0
    eye_p = (lag == 0).astype(F32)
    eye_left = (li == ri)
    b16 = ((ci >> 4) == (ri >> 4)).astype(F32)
    b32 = ((ci >> 5) == (ri >> 5)).astype(F32)
    m16 = b16
    m32 = b32 - b16
    m64 = 1.0 - b32

    zpad = jnp.zeros((8, LANES), F32)
    for src, dst in ((q_ref, qpad), (k_ref, kpad), (v_ref, vpad)):
        dst[0:8, :] = zpad
        dst[seq + 8:seq + 16, :] = zpad
        dst[8:seq + 8, :] = src[0]

    def blockdiag(p16):
        return jnp.concatenate([jnp.where(left, p16, 0), jnp.where(left, 0, p16)], axis=0)

    def pmm(a_list, b_list):
        a_sp = [_split2(a) for a in a_list]
        b_sp = [_split2(b) for b in b_list]
        r1 = [_dot(jnp.concatenate([ah, al], axis=0), blockdiag(bh))
              for (ah, al), (bh, _) in zip(a_sp, b_sp)]
        r2 = [_dot(ah, blockdiag(bl)) for (ah, _), (_, bl) in zip(a_sp, b_sp)]
        return [x[:C] + x[C:] + y for x, y in zip(r1, r2)]

    def lane_pick(x, col):
        return jnp.sum(jnp.where(lane == col, x, 0.0), axis=-1, keepdims=True)

    def chunk_prep(c):
        base = pl.multiple_of(c * C, C)
        base2 = pl.multiple_of(c * 2 * C, 2 * C)
        stores = []

        def conv_silu(pad_ref, w_ref):
            blk = pad_ref[pl.ds(base, C + 16), :]
            y = (blk[7:7 + C] * w_ref[0:1, :] + blk[8:8 + C] * w_ref[1:2, :]
                 + blk[9:9 + C] * w_ref[2:3, :])
            return _silu(y)

        q = conv_silu(qpad, wq_ref)
        k = conv_silu(kpad, wk_ref)
        v = conv_silu(vpad, wv_ref)
        q = q * lax.rsqrt(jnp.sum(q * q, axis=-1, keepdims=True) + NORM_EPS) * (DN_HD ** -0.5)
        k = k * lax.rsqrt(jnp.sum(k * k, axis=-1, keepdims=True) + NORM_EPS)
        k16 = k.astype(BF16)
        gram = _dot_nt(jnp.concatenate([k16, q.astype(BF16)], axis=0), k16)
        gkk = jnp.concatenate([gram[:C], gram[:C]], axis=1)
        gqk = jnp.concatenate([gram[C:], gram[C:]], axis=1)

        gat = gate_ref[0, pl.ds(base, C), :]
        beta, gcol, grow, glast = [], [], [], []
        for d in range(N_DIR):
            beta.append(lane_pick(gat, d * N_DN_HEADS + head))
            g = lane_pick(gat, N_DIR * N_DN_HEADS + d * N_DN_HEADS + head)
            gb = jnp.broadcast_to(g, (C, LANES))
            inc = (ri <= ci) if d == 0 else (ri >= ci)
            gr = jnp.sum(jnp.where(inc, gb, 0.0), axis=0, keepdims=True)
            grow.append(gr)
            gcol.append(jnp.sum(jnp.where(eye_left, jnp.broadcast_to(gr, (C, LANES)), 0.0),
                                axis=-1, keepdims=True))
            glast.append(jnp.sum(g, axis=0, keepdims=True))
        beta_p = jnp.where(fwd_half, beta[0], beta[1])
        diff = jnp.where(fwd_half, gcol[0] - grow[0], gcol[1] - grow[1])
        decay = jnp.exp(jnp.where(causal_p, diff, MASK_NEG))
        l_p = jnp.where(strict_p, beta_p * gkk * decay, 0.0)
        intra = jnp.where(causal_p, gqk * decay, 0.0).astype(BF16)

        rhs, kd, qd = [], [], []
        for d in range(N_DIR):
            eg = jnp.exp(gcol[d])
            kb = k * beta[d]
            rhs.append(jnp.concatenate([v * beta[d], kb * eg], axis=1))
            kd.append((k * jnp.exp(glast[d] - gcol[d])).astype(BF16))
            qd.append(q * eg)
            stores.append((eg_s, (d, pl.ds(pl.multiple_of(c * 8, 8), 8)),
                           jnp.broadcast_to(jnp.exp(glast[d]), (8, LANES))))
        return dict(l_p=l_p, rhs=jnp.concatenate(rhs, axis=0), intra=intra, kd=kd, qd=qd,
                    stores=stores, c=c)

    def phase_a_group(i, carry):
        preps = [chunk_prep(i * GDN_UNROLL + j) for j in range(GDN_UNROLL)]
        l_ps = [p["l_p"] for p in preps]
        x1 = [-l_p * m16 for l_p in l_ps]
        x2 = pmm(x1, x1)
        t = [eye_p + x for x in x1]
        t = [a + b for a, b in zip(t, pmm(t, x2))]
        x4 = pmm(x2, x2)
        t = [a + b for a, b in zip(t, pmm(t, x4))]
        x8 = pmm(x4, x4)
        t = [a + b for a, b in zip(t, pmm(t, x8))]
        for m in (m32, m64):
            t = [a - b for a, b in zip(t, pmm(pmm(t, [l_p * m for l_p in l_ps]), t))]

        t_sp = [_split2(a) for a in t]
        r_sp = [_split2(p["rhs"]) for p in preps]
        uw1 = [_dot(blockdiag(th), jnp.concatenate([rh, rl], axis=1))
               for (th, _), (rh, rl) in zip(t_sp, r_sp)]
        uw2 = [_dot(blockdiag(tl), rh) for (_, tl), (rh, _) in zip(t_sp, r_sp)]
        uw = [(a[:, :2 * LANES] + a[:, 2 * LANES:] + b).astype(BF16) for a, b in zip(uw1, uw2)]

        iw = [_dot(blockdiag(p["intra"]), x) for p, x in zip(preps, uw)]
        kt = [[_dot_tn(p["kd"][d], x[d * C:(d + 1) * C]) for d in range(N_DIR)]
              for p, x in zip(preps, uw)]
        for p, w_iw, w_kt in zip(preps, iw, kt):
            c = p["c"]
            for d in range(N_DIR):
                rows_aq = pl.multiple_of(c * AQ_ROWS, C)
                b_s[d, pl.ds(pl.multiple_of(c * DN_HD, DN_HD), DN_HD), :] = w_kt[d][:, :LANES]
                aq_s[d, pl.ds(rows_aq, DN_HD), :] = (-w_kt[d][:, LANES:]).astype(BF16)
                aq_s[d, pl.ds(rows_aq + DN_HD, C), :] = (
                    p["qd"][d] - w_iw[d * C:(d + 1) * C, LANES:]).astype(BF16)
            o_s[pl.ds(pl.multiple_of(c * C, C), C), :] = w_iw[:C, :LANES] + w_iw[C:, :LANES]
            for ref, idx, val in p["stores"]:
                ref[idx + (slice(None),)] = val
        return carry

    lax.fori_loop(0, n_chunk // GDN_UNROLL, phase_a_group, 0)

    def phase_b(n, states):
        new_states = []
        for d in range(N_DIR):
            c = n if d == 0 else n_chunk - 1 - n
            s = states[d]
            r = _dot(aq_s[d, pl.ds(pl.multiple_of(c * AQ_ROWS, C), AQ_ROWS), :], s.astype(BF16))
            o_s[pl.ds(pl.multiple_of(c * C, C), C), :] += r[DN_HD:]
            eg = eg_s[d, pl.ds(pl.multiple_of(c * 8, 8), 1), :]
            new_states.append(s * eg + r[:DN_HD]
                              + b_s[d, pl.ds(pl.multiple_of(c * DN_HD, DN_HD), DN_HD), :])
        return tuple(new_states)

    zero = jnp.zeros((DN_HD, DN_HD), F32)
    lax.fori_loop(0, n_chunk, phase_b, (zero, zero))

    def finish(c, carry):
        rs = pl.ds(pl.multiple_of(c * C, C), C)
        o = o_s[rs, :]
        o = o * lax.rsqrt(jnp.mean(o * o, axis=-1, keepdims=True) + NORM_EPS) * gn_ref[...]
        o_ref[0, rs, :] = (o * _silu(z_ref[0, rs, :].astype(F32))).astype(BF16)
        return carry

    lax.fori_loop(0, n_chunk, finish, 0, unroll=2)


def _gdn_call(dn, z, gates, conv_w, out_norm, batch, seq):
    dn3 = dn.reshape(batch, seq, 3 * DN_W)
    z3 = z.reshape(batch, seq, DN_W)
    g3 = gates.reshape(batch, seq, GATE_PAD)
    blk = (1, seq, LANES)
    nh = N_DN_HEADS
    n_chunk = seq // CHUNK
    return pl.pallas_call(
        functools.partial(_gdn_kernel, seq=seq),
        grid=(batch, nh),
        in_specs=[pl.BlockSpec(blk, lambda b, h: (b, 0, h)),
                  pl.BlockSpec(blk, lambda b, h: (b, 0, nh + h)),
                  pl.BlockSpec(blk, lambda b, h: (b, 0, 2 * nh + h)),
                  pl.BlockSpec(blk, lambda b, h: (b, 0, h)),
                  pl.BlockSpec(blk, lambda b, h: (b, 0, 0)),
                  pl.BlockSpec((3, LANES), lambda b, h: (0, h)),
                  pl.BlockSpec((3, LANES), lambda b, h: (0, nh + h)),
                  pl.BlockSpec((3, LANES), lambda b, h: (0, 2 * nh + h)),
                  _const_spec((1, LANES))],
        out_specs=pl.BlockSpec(blk, lambda b, h: (b, 0, h)),
        out_shape=jax.ShapeDtypeStruct((batch, seq, DN_W), BF16),
        scratch_shapes=[pltpu.VMEM((seq + 16, LANES), F32)] * 3 + [
            pltpu.VMEM((N_DIR, AQ_ROWS * n_chunk, LANES), BF16),
            pltpu.VMEM((N_DIR, DN_HD * n_chunk, LANES), F32),
            pltpu.VMEM((N_DIR, 8 * n_chunk, LANES), F32),
            pltpu.VMEM((seq, LANES), F32)],
        compiler_params=_cparams(2),
        name="gdn",
    )(dn3, dn3, dn3, z3, g3, conv_w, conv_w, conv_w, out_norm)


def _oproj_kernel(att_ref, dn_ref, x_ref, mod_ref, g_ref, w_ref, x1_ref, h_ref):
    mix = _dot(att_ref[...], w_ref[0:ATT_W, :]) + _dot(dn_ref[...], w_ref[ATT_W:, :])
    x1 = x_ref[...] + mod_ref[0, 2:3, :] * mix
    x1_ref[...] = x1
    ms = jnp.mean(x1 * x1, axis=-1, keepdims=True)
    y = x1 * lax.rsqrt(ms + NORM_EPS) * g_ref[...]
    h_ref[...] = (y * (1.0 + mod_ref[0, 4:5, :]) + mod_ref[0, 3:4, :]).astype(BF16)


def _oproj_call(att, dno, x2, mod3, g2, w_o, seq):
    n_tok = x2.shape[0]
    tm = TOKEN_TILE
    per = seq // tm
    tok = lambda i: (i, 0)
    return pl.pallas_call(
        _oproj_kernel,
        grid=(n_tok // tm,),
        in_specs=[pl.BlockSpec((tm, ATT_W), tok),
                  pl.BlockSpec((tm, DN_W), tok),
                  pl.BlockSpec((tm, D_MODEL), tok),
                  pl.BlockSpec((1, 6, D_MODEL), lambda i: (i // per, 0, 0)),
                  _const_spec((1, D_MODEL)),
                  _const_spec((D_MODEL, D_MODEL))],
        out_specs=[pl.BlockSpec((tm, D_MODEL), tok),
                   pl.BlockSpec((tm, D_MODEL), tok)],
        out_shape=[jax.ShapeDtypeStruct((n_tok, D_MODEL), F32),
                   jax.ShapeDtypeStruct((n_tok, D_MODEL), BF16)],
        compiler_params=_cparams(1),
        name="oproj",
    )(att, dno, x2, mod3, g2, w_o)


def _ffn_kernel(h_ref, hp_ref, hn_ref, x1_ref, mod_ref, wu_ref, cw_ref, cb_ref, wd_ref, y_ref,
                *, per):
    i = pl.program_id(0)
    tm = h_ref.shape[0]
    zero = jnp.zeros((HALO, D_MODEL), BF16)
    hp = jnp.where(i % per != 0, hp_ref[...], zero)
    hn = jnp.where(i % per != per - 1, hn_ref[...], zero)
    hx = jnp.concatenate([hp, h_ref[...], hn], axis=0)
    acc = jnp.zeros((tm, D_MODEL), F32)
    for j in range(D_FF // FFN_CHUNK):
        halves = []
        for o in (j * FFN_CHUNK, D_FF + j * FFN_CHUNK):
            u = _dot(hx, wu_ref[:, o:o + FFN_CHUNK])
            cw = cw_ref[:, o:o + FFN_CHUNK]
            halves.append(u[HALO - 1:HALO - 1 + tm] * cw[0:1, :] + u[HALO:HALO + tm] * cw[1:2, :]
                          + u[HALO + 1:HALO + 1 + tm] * cw[2:3, :] + cb_ref[:, o:o + FFN_CHUNK])
        act = (_silu(halves[0]) * halves[1]).astype(BF16)
        acc = acc + _dot(act, wd_ref[j * FFN_CHUNK:(j + 1) * FFN_CHUNK, :])
    y_ref[...] = x1_ref[...] + mod_ref[0, 5:6, :] * acc


def _ffn_call(h2, x1, mod3, w_up, conv_w, conv_b, w_down, seq):
    n_tok = x1.shape[0]
    tm = TOKEN_TILE
    per = seq // tm
    hb = tm // HALO
    last = n_tok // HALO - 1
    tok = lambda i: (i, 0)
    return pl.pallas_call(
        functools.partial(_ffn_kernel, per=per),
        grid=(n_tok // tm,),
        in_specs=[pl.BlockSpec((tm, D_MODEL), tok),
                  pl.BlockSpec((HALO, D_MODEL), lambda i: (jnp.maximum(i * hb - 1, 0), 0)),
                  pl.BlockSpec((HALO, D_MODEL), lambda i: (jnp.minimum((i + 1) * hb, last), 0)),
                  pl.BlockSpec((tm, D_MODEL), tok),
                  pl.BlockSpec((1, 6, D_MODEL), lambda i: (i // per, 0, 0)),
                  _const_spec((D_MODEL, 2 * D_FF)),
                  _const_spec((3, 2 * D_FF)),
                  _const_spec((1, 2 * D_FF)),
                  _const_spec((D_FF, D_MODEL))],
        out_specs=pl.BlockSpec((tm, D_MODEL), tok),
        out_shape=jax.ShapeDtypeStruct((n_tok, D_MODEL), F32),
        compiler_params=_cparams(1),
        name="ffn",
    )(h2, h2, h2, x1, mod3, w_up, conv_w, conv_b, w_down)


def _trunk(x, mod, p):
    batch, seq, d = x.shape
    assert seq % TOKEN_TILE == 0 and seq % GRID_W == 0
    x2 = x.reshape(batch * seq, d)
    mod3 = mod.reshape(batch, 6, d)
    att, dn, z, gates = _inproj_call(x2, mod3, p["norm1_g"], p["w_in"], p["alog_l"], p["dtb_l"], seq)
    att_o = _natten_call(att, p["bias_tbl"], p["qg2"], p["kg2"], p["ones_bd"], batch, seq)
    dn_o = _gdn_call(dn, z, gates, p["dn_conv_w"], p["dn_out_norm"], batch, seq)
    x1, h2 = _oproj_call(att_o.reshape(batch * seq, ATT_W), dn_o.reshape(batch * seq, DN_W),
                         x2, mod3, p["norm2_g"], p["w_o"], seq)
    y = _ffn_call(h2, x1, mod3, p["ffn_w_up"], p["ffn_conv_w"], p["ffn_conv_b"], p["ffn_w_down"], seq)
    return y.reshape(batch, seq, d)


def kernel(x_prompt, x_sample, c_prompt, c_sample, ada_w, ada_b, norm1_g, norm2_g, w_in,
           att_q_norm, att_k_norm, att_rpb, dn_conv_w, dn_a_log, dn_dt_bias, dn_out_norm, w_o,
           ffn_w_up, ffn_conv_w, ffn_conv_b, ffn_w_down):
    depth = ada_w.shape[0]
    bp, bs = c_prompt.shape[0], c_sample.shape[0]
    n_c = -(-(bp + bs) // 8) * 8
    gate_lanes = lambda a: jnp.pad(a.reshape(1, N_DIR * N_DN_HEADS),
                                   ((0, 0), (N_DIR * N_DN_HEADS, GATE_PAD - N_GATE)))
    blk = np.arange(LANES) // ATT_HD
    ones_bd = jnp.asarray(blk[:, None] == blk[None, :], BF16)
    xs = [x_prompt, x_sample]
    for l in range(depth):
        p = {
            "norm1_g": norm1_g[l].reshape(1, -1), "norm2_g": norm2_g[l].reshape(1, -1),
            "w_in": jnp.pad(w_in[l], ((0, 0), (0, GATE_PAD - N_GATE))).astype(BF16),
            "alog_l": gate_lanes(dn_a_log[l]), "dtb_l": gate_lanes(dn_dt_bias[l]),
            "bias_tbl": _natten_bias_table(att_rpb[l]),
            "qg2": jnp.tile(att_q_norm[l], 2).reshape(1, LANES),
            "kg2": jnp.tile(att_k_norm[l], 2).reshape(1, LANES),
            "ones_bd": ones_bd,
            "dn_conv_w": dn_conv_w[l], "dn_out_norm": dn_out_norm[l].reshape(1, -1),
            "w_o": w_o[l].astype(BF16), "ffn_w_up": ffn_w_up[l].astype(BF16),
            "ffn_conv_w": ffn_conv_w[l], "ffn_conv_b": ffn_conv_b[l].reshape(1, -1),
            "ffn_w_down": ffn_w_down[l].astype(BF16),
        }
        c_all = jnp.pad(jnp.concatenate([c_prompt, c_sample], axis=0), ((0, n_c - bp - bs), (0, 0)))
        mod = _mod_call(c_all, ada_w[l], ada_b[l])
        xs = [_trunk(xs[0], mod[:bp], p), _trunk(xs[1], mod[bp:bp + bs], p)]
    return tuple(xs)
```

```python
import functools

import numpy as np
import jax
import jax.numpy as jnp
from jax import lax
from jax.experimental import pallas as pl
from jax.experimental.pallas import tpu as pltpu

F32 = jnp.float32
BF16 = jnp.bfloat16

D_MODEL = 1024
GRID_W = 64
N_ATT_HEADS = 8
ATT_HD = 64
ATT_W = N_ATT_HEADS * ATT_HD
WIN_R = 8
WIN_C = 16
N_DN_HEADS = 4
DN_HD = 128
DN_W = N_DN_HEADS * DN_HD
N_DIR = 2
CHUNK = 64
D_FF = 2816
NORM_EPS = 1e-6
N_GATE = 2 * N_DIR * N_DN_HEADS
LANES = 128
GATE_PAD = LANES
IN_COLS_PAD = 3 * ATT_W + 4 * DN_W + GATE_PAD
MASK_NEG = -1e30

VMEM_LIMIT = 56 * 1024 * 1024
TOKEN_TILE = 512
FFN_CHUNK = 256
HALO = 16
AQ_ROWS = DN_HD + CHUNK
NATTEN_UNROLL = 8
GDN_UNROLL = 16


def _cparams(n_axes):
    return pltpu.CompilerParams(dimension_semantics=("arbitrary",) * n_axes,
                                vmem_limit_bytes=VMEM_LIMIT)


def _const_spec(shape):
    nd = len(shape)
    return pl.BlockSpec(shape, lambda *_: (0,) * nd, pipeline_mode=pl.Buffered(1))


def _silu(x):
    return x * jax.nn.sigmoid(x)


def _dot(a, b):
    return jnp.dot(a, b, preferred_element_type=F32)


def _dot_nt(a, b):
    return lax.dot_general(a, b, (((1,), (1,)), ((), ())), preferred_element_type=F32)


def _dot_tn(a, b):
    return lax.dot_general(a, b, (((0,), (0,)), ((), ())), preferred_element_type=F32)


def _split2(x):
    hi = x.astype(BF16)
    lo = (x - hi.astype(F32)).astype(BF16)
    return hi, lo


def _split3(x):
    hi = x.astype(BF16)
    r = x - hi.astype(F32)
    mid = r.astype(BF16)
    lo = (r - mid.astype(F32)).astype(BF16)
    return hi, mid, lo


def _mod_kernel(c_ref, w_ref, b_ref, o_ref):
    s = _silu(c_ref[...])
    o_ref[...] = jnp.dot(s, w_ref[...], preferred_element_type=F32,
                         precision=lax.Precision.HIGHEST) + b_ref[...]


def _mod_call(c, ada_w, ada_b):
    bp, d = c.shape
    n = ada_w.shape[1]
    bn = 1024
    return pl.pallas_call(
        _mod_kernel,
        grid=(n // bn,),
        in_specs=[pl.BlockSpec((bp, d), lambda j: (0, 0)),
                  pl.BlockSpec((d, bn), lambda j: (0, j)),
                  pl.BlockSpec((1, bn), lambda j: (0, j))],
        out_specs=pl.BlockSpec((bp, bn), lambda j: (0, j)),
        out_shape=jax.ShapeDtypeStruct((bp, n), F32),
        compiler_params=_cparams(1),
        name="mod",
    )(c, ada_w, ada_b.reshape(1, n))


def _inproj_kernel(x_ref, mod_ref, g_ref, w_ref, alog_ref, dtb_ref,
                   att_ref, dn_ref, z_ref, gate_ref):
    x = x_ref[...]
    ms = jnp.mean(x * x, axis=-1, keepdims=True)
    y = x * lax.rsqrt(ms + NORM_EPS) * g_ref[...]
    h = (y * (1.0 + mod_ref[0, 1:2, :]) + mod_ref[0, 0:1, :]).astype(BF16)
    cw = 512
    for j in range(3 * ATT_W // cw):
        att_ref[:, j * cw:(j + 1) * cw] = _dot(h, w_ref[:, j * cw:(j + 1) * cw]).astype(BF16)
    o = 3 * ATT_W
    for j in range(3 * DN_W // cw):
        dn_ref[:, j * cw:(j + 1) * cw] = _dot(h, w_ref[:, o + j * cw:o + (j + 1) * cw])
    o += 3 * DN_W
    z_ref[...] = _dot(h, w_ref[:, o:o + DN_W]).astype(BF16)
    o += DN_W
    graw = _dot(h, w_ref[:, o:o + GATE_PAD])
    lane = lax.broadcasted_iota(jnp.int32, (1, GATE_PAD), 1)
    t = graw + dtb_ref[...]
    softplus = jnp.maximum(t, 0.0) + jnp.log1p(jnp.exp(-jnp.abs(t)))
    decay = -jnp.exp(alog_ref[...]) * softplus
    gate_ref[...] = jnp.where(lane < N_DIR * N_DN_HEADS, jax.nn.sigmoid(graw), decay)


def _inproj_call(x2, mod3, g1, w_cat, alog_l, dtb_l, seq):
    n_tok = x2.shape[0]
    tm = TOKEN_TILE
    per = seq // tm
    tok = lambda i: (i, 0)
    return pl.pallas_call(
        _inproj_kernel,
        grid=(n_tok // tm,),
        in_specs=[pl.BlockSpec((tm, D_MODEL), tok),
                  pl.BlockSpec((1, 6, D_MODEL), lambda i: (i // per, 0, 0)),
                  _const_spec((1, D_MODEL)),
                  _const_spec((D_MODEL, IN_COLS_PAD)),
                  _const_spec((1, GATE_PAD)),
                  _const_spec((1, GATE_PAD))],
        out_specs=[pl.BlockSpec((tm, 3 * ATT_W), tok),
                   pl.BlockSpec((tm, 3 * DN_W), tok),
                   pl.BlockSpec((tm, DN_W), tok),
                   pl.BlockSpec((tm, GATE_PAD), tok)],
        out_shape=[jax.ShapeDtypeStruct((n_tok, 3 * ATT_W), BF16),
                   jax.ShapeDtypeStruct((n_tok, 3 * DN_W), F32),
                   jax.ShapeDtypeStruct((n_tok, DN_W), BF16),
                   jax.ShapeDtypeStruct((n_tok, GATE_PAD), F32)],
        compiler_params=_cparams(1),
        name="inproj",
    )(x2, mod3, g1, w_cat, alog_l, dtb_l)


def _natten_kernel(q_ref, k_ref, v_ref, bias_ref, qg_ref, kg_ref, ones_ref, o_ref,
                   qa_s, qb_s, kn_s, *, rows):
    lane = lax.broadcasted_iota(jnp.int32, (1, LANES), 1)
    first = lane < ATT_HD

    def head_norm(x, g):
        ss = _dot((x * x).astype(BF16), ones_ref[...])
        return x * lax.rsqrt(ss * (1.0 / ATT_HD) + NORM_EPS) * g

    def prep(r, carry):
        rs = pl.ds(pl.multiple_of(r * GRID_W, GRID_W), GRID_W)
        q = head_norm(q_ref[0, rs, :].astype(F32), qg_ref[...]) * (ATT_HD ** -0.5)
        qa_s[rs, :] = jnp.where(first, q, 0.0).astype(BF16)
        qb_s[rs, :] = jnp.where(first, 0.0, q).astype(BF16)
        kn_s[rs, :] = head_norm(k_ref[0, rs, :].astype(F32), kg_ref[...]).astype(BF16)
        return carry

    lax.fori_loop(0, rows, prep, 0, unroll=2)

    def row_group(i, carry):
        units = []
        for j in range(NATTEN_UNROLL):
            r = i * NATTEN_UNROLL + j
            r0 = jnp.clip(r - WIN_R // 2, 0, rows - WIN_R)
            qs = pl.ds(pl.multiple_of(r * GRID_W, GRID_W), GRID_W)
            ks = pl.ds(pl.multiple_of(r0 * GRID_W, GRID_W), WIN_R * GRID_W)
            units.append((qs, ks, r - r0))
        scores = [[_dot_nt(q_s[qs, :], kn_s[ks, :]) + bias_ref[hh, dd]
                   for hh, q_s in enumerate((qa_s, qb_s))] for qs, ks, dd in units]
        probs = []
        for row_scores in scores:
            row_probs = []
            for s in row_scores:
                p = jnp.exp(s - jnp.max(s, axis=-1, keepdims=True))
                row_probs.append((p.astype(BF16), jnp.sum(p, axis=-1, keepdims=True)))
            probs.append(row_probs)
        for (qs, ks, _), row_probs in zip(units, probs):
            vb = v_ref[0, ks, :]
            oa, ob = [_dot(p, vb) / l for p, l in row_probs]
            o_ref[0, qs, :] = jnp.where(first, oa, ob).astype(BF16)
        return carry

    lax.fori_loop(0, rows // NATTEN_UNROLL, row_group, 0)


def _natten_call(att, bias_tbl, qg2, kg2, ones_bd, batch, seq):
    rows = seq // GRID_W
    assert rows >= WIN_R
    att3 = att.reshape(batch, seq, 3 * ATT_W)
    n_pair = N_ATT_HEADS // 2
    blk = (1, seq, LANES)
    return pl.pallas_call(
        functools.partial(_natten_kernel, rows=rows),
        grid=(n_pair, batch),
        in_specs=[pl.BlockSpec(blk, lambda p, b: (b, 0, p)),
                  pl.BlockSpec(blk, lambda p, b: (b, 0, n_pair + p)),
                  pl.BlockSpec(blk, lambda p, b: (b, 0, 2 * n_pair + p)),
                  pl.BlockSpec((2, WIN_R, GRID_W, WIN_R * GRID_W), lambda p, b: (p, 0, 0, 0)),
                  _const_spec((1, LANES)),
                  _const_spec((1, LANES)),
                  _const_spec((LANES, LANES))],
        out_specs=pl.BlockSpec(blk, lambda p, b: (b, 0, p)),
        out_shape=jax.ShapeDtypeStruct((batch, seq, ATT_W), BF16),
        scratch_shapes=[pltpu.VMEM((seq, LANES), BF16)] * 3,
        compiler_params=_cparams(2),
        name="natten",
    )(att3, att3, att3, bias_tbl, qg2, kg2, ones_bd)


def _natten_bias_table(rpb):
    n_row = 2 * WIN_R - 1
    per = 2 * GRID_W
    v = jnp.zeros((N_ATT_HEADS, n_row, per), F32)
    v = v.at[:, :, :WIN_C].set(rpb[:, :, WIN_C - 1:])
    v = v.at[:, :, per - (WIN_C - 1):].set(rpb[:, :, :WIN_C - 1])
    toep = jnp.tile(v, (1, 1, GRID_W))[:, :, :GRID_W * (per - 1)]
    toep = toep.reshape(N_ATT_HEADS, n_row, GRID_W, per - 1)[:, :, :, :GRID_W]
    qc = np.arange(GRID_W)[:, None]
    kc = np.arange(GRID_W)[None, :]
    c0 = np.clip(qc - WIN_C // 2, 0, GRID_W - WIN_C)
    valid = (kc >= c0) & (kc < c0 + WIN_C)
    toep = jnp.where(valid[None, None], toep, MASK_NEG)
    bands = [jnp.swapaxes(toep[:, WIN_R - 1 - dd:2 * WIN_R - 1 - dd], 1, 2) for dd in range(WIN_R)]
    tbl = jnp.stack(bands, axis=1)
    return tbl.reshape(N_ATT_HEADS, WIN_R, GRID_W, WIN_R * GRID_W)


def _gdn_kernel(q_ref, k_ref, v_ref, z_ref, gate_ref, wq_ref, wk_ref, wv_ref, gn_ref, o_ref,
                qpad, kpad, vpad, aq_s, b_s, eg_s, o_s, *, seq):
    n_chunk = seq // CHUNK
    head = pl.program_id(1)
    C = CHUNK
    lane = lax.broadcasted_iota(jnp.int32, (1, LANES), 1)
    left = lane < C
    ri = lax.broadcasted_iota(jnp.int32, (C, LANES), 0)
    li = lax.broadcasted_iota(jnp.int32, (C, LANES), 1)
    ci = li & (C - 1)
    fwd_half = li < C
    lag = jnp.where(fwd_half, ri - ci, ci - ri)
    causal_p = lag >= 0
    strict_p = lag > 0
    eye_p = (lag == 0).astype(F32)
    eye_left = (li == ri)
    b16 = ((ci >> 4) == (ri >> 4)).astype(F32)
    b32 = ((ci >> 5) == (ri >> 5)).astype(F32)
    m16 = b16
    m32 = b32 - b16
    m64 = 1.0 - b32

    zpad = jnp.zeros((8, LANES), F32)
    for src, dst in ((q_ref, qpad), (k_ref, kpad), (v_ref, vpad)):
        dst[0:8, :] = zpad
        dst[seq + 8:seq + 16, :] = zpad
        dst[8:seq + 8, :] = src[0]

    def blockdiag(p16):
        return jnp.concatenate([jnp.where(left, p16, 0), jnp.where(left, 0, p16)], axis=0)

    def pmm(a_list, b_list):
        return [_dot(a.astype(BF16), blockdiag(b.astype(BF16))) for a, b in zip(a_list, b_list)]

    def lane_pick(x, col):
        return jnp.sum(jnp.where(lane == col, x, 0.0), axis=-1, keepdims=True)

    def chunk_prep(c):
        base = pl.multiple_of(c * C, C)
        base2 = pl.multiple_of(c * 2 * C, 2 * C)
        stores = []

        def conv_silu(pad_ref, w_ref):
            blk = pad_ref[pl.ds(base, C + 16), :]
            prev = pltpu.roll(blk, 1, 0)[8:8 + C]
            nxt = pltpu.roll(blk, C + 15, 0)[8:8 + C]
            y = prev * w_ref[0:1, :] + blk[8:8 + C] * w_ref[1:2, :] + nxt * w_ref[2:3, :]
            return _silu(y)

        q = conv_silu(qpad, wq_ref)
        k = conv_silu(kpad, wk_ref)
        v = conv_silu(vpad, wv_ref)
        q = q * lax.rsqrt(jnp.sum(q * q, axis=-1, keepdims=True) + NORM_EPS) * (DN_HD ** -0.5)
        k = k * lax.rsqrt(jnp.sum(k * k, axis=-1, keepdims=True) + NORM_EPS)
        k16 = k.astype(BF16)
        gram = _dot_nt(jnp.concatenate([k16, q.astype(BF16)], axis=0), k16)
        gkk = jnp.concatenate([gram[:C], gram[:C]], axis=1)
        gqk = jnp.concatenate([gram[C:], gram[C:]], axis=1)

        gat = gate_ref[0, pl.ds(base, C), :]
        beta, gcol, grow, glast = [], [], [], []
        for d in range(N_DIR):
            beta.append(lane_pick(gat, d * N_DN_HEADS + head))
            g = lane_pick(gat, N_DIR * N_DN_HEADS + d * N_DN_HEADS + head)
            gb = jnp.broadcast_to(g, (C, LANES))
            inc = (ri <= ci) if d == 0 else (ri >= ci)
            gr = jnp.sum(jnp.where(inc, gb, 0.0), axis=0, keepdims=True)
            grow.append(gr)
            gcol.append(jnp.sum(jnp.where(eye_left, jnp.broadcast_to(gr, (C, LANES)), 0.0),
                                axis=-1, keepdims=True))
            glast.append(jnp.sum(g, axis=0, keepdims=True))
        beta_p = jnp.where(fwd_half, beta[0], beta[1])
        diff = jnp.where(fwd_half, gcol[0] - grow[0], gcol[1] - grow[1])
        decay = jnp.exp(jnp.where(causal_p, diff, MASK_NEG))
        l_p = jnp.where(strict_p, beta_p * gkk * decay, 0.0)
        intra = jnp.where(causal_p, gqk * decay, 0.0).astype(BF16)

        rhs, kd, qd = [], [], []
        for d in range(N_DIR):
            eg = jnp.exp(gcol[d])
            kb = k * beta[d]
            rhs.append(jnp.concatenate([v * beta[d], kb * eg], axis=1))
            kd.append((k * jnp.exp(glast[d] - gcol[d])).astype(BF16))
            qd.append(q * eg)
            stores.append((eg_s, (d, pl.ds(pl.multiple_of(c * 8, 8), 8)),
                           jnp.broadcast_to(jnp.exp(glast[d]), (8, LANES))))
        return dict(l_p=l_p, rhs=jnp.concatenate(rhs, axis=0), intra=intra, kd=kd, qd=qd,
                    stores=stores, c=c)

    def phase_a_group(i, carry):
        preps = [chunk_prep(i * GDN_UNROLL + j) for j in range(GDN_UNROLL)]
        l_ps = [p["l_p"] for p in preps]
        add = lambda xs, ys: [x + y for x, y in zip(xs, ys)]
        n = [-l_p * m16 for l_p in l_ps]
        xp = n
        for _ in range(3):
            xp = pmm(xp, xp)
            n = add(add(n, xp), pmm(n, xp))
        for m in (m32, m64):
            cm = [l_p * m for l_p in l_ps]
            y = add(cm, pmm(n, cm))
            n = [a - b - c for a, b, c in zip(n, y, pmm(y, n))]

        uw = [(p["rhs"] + _dot(blockdiag(a.astype(BF16)), p["rhs"].astype(BF16))).astype(BF16)
              for p, a in zip(preps, n)]

        iw = [_dot(blockdiag(p["intra"]), x) for p, x in zip(preps, uw)]
        kt = [[_dot_tn(p["kd"][d], x[d * C:(d + 1) * C]) for d in range(N_DIR)]
              for p, x in zip(preps, uw)]
        for p, w_iw, w_kt in zip(preps, iw, kt):
            c = p["c"]
            for d in range(N_DIR):
                rows_aq = pl.multiple_of(c * AQ_ROWS, C)
                b_s[d, pl.ds(pl.multiple_of(c * DN_HD, DN_HD), DN_HD), :] = w_kt[d][:, :LANES]
                aq_s[d, pl.ds(rows_aq, DN_HD), :] = (-w_kt[d][:, LANES:]).astype(BF16)
                aq_s[d, pl.ds(rows_aq + DN_HD, C), :] = (
                    p["qd"][d] - w_iw[d * C:(d + 1) * C, LANES:]).astype(BF16)
            o_s[pl.ds(pl.multiple_of(c * C, C), C), :] = w_iw[:C, :LANES] + w_iw[C:, :LANES]
            for ref, idx, val in p["stores"]:
                ref[idx + (slice(None),)] = val
        return carry

    lax.fori_loop(0, n_chunk // GDN_UNROLL, phase_a_group, 0)

    def phase_b(n, states):
        new_states = []
        for d in range(N_DIR):
            c = n if d == 0 else n_chunk - 1 - n
            s = states[d]
            r = _dot(aq_s[d, pl.ds(pl.multiple_of(c * AQ_ROWS, C), AQ_ROWS), :], s.astype(BF16))
            o_s[pl.ds(pl.multiple_of(c * C, C), C), :] += r[DN_HD:]
            eg = eg_s[d, pl.ds(pl.multiple_of(c * 8, 8), 1), :]
            new_states.append(s * eg + r[:DN_HD]
                              + b_s[d, pl.ds(pl.multiple_of(c * DN_HD, DN_HD), DN_HD), :])
        return tuple(new_states)

    zero = jnp.zeros((DN_HD, DN_HD), F32)
    lax.fori_loop(0, n_chunk, phase_b, (zero, zero))

    def finish(c, carry):
        rs = pl.ds(pl.multiple_of(c * C, C), C)
        o = o_s[rs, :]
        o = o * lax.rsqrt(jnp.mean(o * o, axis=-1, keepdims=True) + NORM_EPS) * gn_ref[...]
        o_ref[0, rs, :] = (o * _silu(z_ref[0, rs, :].astype(F32))).astype(BF16)
        return carry

    lax.fori_loop(0, n_chunk, finish, 0, unroll=2)


def _gdn_call(dn, z, gates, conv_w, out_norm, batch, seq):
    dn3 = dn.reshape(batch, seq, 3 * DN_W)
    z3 = z.reshape(batch, seq, DN_W)
    g3 = gates.reshape(batch, seq, GATE_PAD)
    blk = (1, seq, LANES)
    nh = N_DN_HEADS
    n_chunk = seq // CHUNK
    return pl.pallas_call(
        functools.partial(_gdn_kernel, seq=seq),
        grid=(batch, nh),
        in_specs=[pl.BlockSpec(blk, lambda b, h: (b, 0, h)),
                  pl.BlockSpec(blk, lambda b, h: (b, 0, nh + h)),
                  pl.BlockSpec(blk, lambda b, h: (b, 0, 2 * nh + h)),
                  pl.BlockSpec(blk, lambda b, h: (b, 0, h)),
                  pl.BlockSpec(blk, lambda b, h: (b, 0, 0)),
                  pl.BlockSpec((3, LANES), lambda b, h: (0, h)),
                  pl.BlockSpec((3, LANES), lambda b, h: (0, nh + h)),
                  pl.BlockSpec((3, LANES), lambda b, h: (0, 2 * nh + h)),
                  _const_spec((1, LANES))],
        out_specs=pl.BlockSpec(blk, lambda b, h: (b, 0, h)),
        out_shape=jax.ShapeDtypeStruct((batch, seq, DN_W), BF16),
        scratch_shapes=[pltpu.VMEM((seq + 16, LANES), F32)] * 3 + [
            pltpu.VMEM((N_DIR, AQ_ROWS * n_chunk, LANES), BF16),
            pltpu.VMEM((N_DIR, DN_HD * n_chunk, LANES), F32),
            pltpu.VMEM((N_DIR, 8 * n_chunk, LANES), F32),
            pltpu.VMEM((seq, LANES), F32)],
        compiler_params=_cparams(2),
        name="gdn",
    )(dn3, dn3, dn3, z3, g3, conv_w, conv_w, conv_w, out_norm)


def _oproj_kernel(att_ref, dn_ref, x_ref, mod_ref, g_ref, w_ref, x1_ref, h_ref):
    mix = _dot(att_ref[...], w_ref[0:ATT_W, :]) + _dot(dn_ref[...], w_ref[ATT_W:, :])
    x1 = x_ref[...] + mod_ref[0, 2:3, :] * mix
    x1_ref[...] = x1
    ms = jnp.mean(x1 * x1, axis=-1, keepdims=True)
    y = x1 * lax.rsqrt(ms + NORM_EPS) * g_ref[...]
    h_ref[...] = (y * (1.0 + mod_ref[0, 4:5, :]) + mod_ref[0, 3:4, :]).astype(BF16)


def _oproj_call(att, dno, x2, mod3, g2, w_o, seq):
    n_tok = x2.shape[0]
    tm = TOKEN_TILE
    per = seq // tm
    tok = lambda i: (i, 0)
    return pl.pallas_call(
        _oproj_kernel,
        grid=(n_tok // tm,),
        in_specs=[pl.BlockSpec((tm, ATT_W), tok),
                  pl.BlockSpec((tm, DN_W), tok),
                  pl.BlockSpec((tm, D_MODEL), tok),
                  pl.BlockSpec((1, 6, D_MODEL), lambda i: (i // per, 0, 0)),
                  _const_spec((1, D_MODEL)),
                  _const_spec((D_MODEL, D_MODEL))],
        out_specs=[pl.BlockSpec((tm, D_MODEL), tok),
                   pl.BlockSpec((tm, D_MODEL), tok)],
        out_shape=[jax.ShapeDtypeStruct((n_tok, D_MODEL), F32),
                   jax.ShapeDtypeStruct((n_tok, D_MODEL), BF16)],
        compiler_params=_cparams(1),
        name="oproj",
    )(att, dno, x2, mod3, g2, w_o)


def _ffn_kernel(h_ref, hp_ref, hn_ref, x1_ref, mod_ref, wu_ref, cw_ref, cb_ref, wd_ref, y_ref,
                u_s, act_s, *, per):
    i = pl.program_id(0)
    tm = h_ref.shape[0]
    zero = jnp.zeros((HALO, D_MODEL), BF16)
    hp = jnp.where(i % per != 0, hp_ref[...], zero)
    hn = jnp.where(i % per != per - 1, hn_ref[...], zero)
    hx = jnp.concatenate([hp, h_ref[...], hn], axis=0)
    n_chunk = D_FF // FFN_CHUNK

    def up(j):
        for half, o in enumerate((j * FFN_CHUNK, D_FF + j * FFN_CHUNK)):
            u_s[j % 2, half] = _dot(hx, wu_ref[:, o:o + FFN_CHUNK])

    def gate(j):
        halves = []
        for half, o in enumerate((j * FFN_CHUNK, D_FF + j * FFN_CHUNK)):
            cw = cw_ref[:, o:o + FFN_CHUNK]
            taps = [u_s[j % 2, half, HALO - 1 + k:HALO - 1 + k + tm, :] for k in range(3)]
            halves.append(taps[0] * cw[0:1, :] + taps[1] * cw[1:2, :] + taps[2] * cw[2:3, :]
                          + cb_ref[:, o:o + FFN_CHUNK])
        act_s[:, j * FFN_CHUNK:(j + 1) * FFN_CHUNK] = (_silu(halves[0]) * halves[1]).astype(BF16)

    up(0)
    for j in range(n_chunk):
        if j + 1 < n_chunk:
            up(j + 1)
        gate(j)
    y_ref[...] = x1_ref[...] + mod_ref[0, 5:6, :] * _dot(act_s[...], wd_ref[...])


def _ffn_call(h2, x1, mod3, w_up, conv_w, conv_b, w_down, seq):
    n_tok = x1.shape[0]
    tm = TOKEN_TILE
    per = seq // tm
    hb = tm // HALO
    last = n_tok // HALO - 1
    tok = lambda i: (i, 0)
    return pl.pallas_call(
        functools.partial(_ffn_kernel, per=per),
        grid=(n_tok // tm,),
        in_specs=[pl.BlockSpec((tm, D_MODEL), tok),
                  pl.BlockSpec((HALO, D_MODEL), lambda i: (jnp.maximum(i * hb - 1, 0), 0)),
                  pl.BlockSpec((HALO, D_MODEL), lambda i: (jnp.minimum((i + 1) * hb, last), 0)),
                  pl.BlockSpec((tm, D_MODEL), tok),
                  pl.BlockSpec((1, 6, D_MODEL), lambda i: (i // per, 0, 0)),
                  _const_spec((D_MODEL, 2 * D_FF)),
                  _const_spec((3, 2 * D_FF)),
                  _const_spec((1, 2 * D_FF)),
                  _const_spec((D_FF, D_MODEL))],
        out_specs=pl.BlockSpec((tm, D_MODEL), tok),
        out_shape=jax.ShapeDtypeStruct((n_tok, D_MODEL), F32),
        scratch_shapes=[pltpu.VMEM((2, 2, tm + 2 * HALO, FFN_CHUNK), F32),
                        pltpu.VMEM((tm, D_FF), BF16)],
        compiler_params=_cparams(1),
        name="ffn",
    )(h2, h2, h2, x1, mod3, w_up, conv_w, conv_b, w_down)


def _trunk(x, mod, p):
    batch, seq, d = x.shape
    assert seq % TOKEN_TILE == 0 and seq % GRID_W == 0
    x2 = x.reshape(batch * seq, d)
    mod3 = mod.reshape(batch, 6, d)
    att, dn, z, gates = _inproj_call(x2, mod3, p["norm1_g"], p["w_in"], p["alog_l"], p["dtb_l"], seq)
    att_o = _natten_call(att, p["bias_tbl"], p["qg2"], p["kg2"], p["ones_bd"], batch, seq)
    dn_o = _gdn_call(dn, z, gates, p["dn_conv_w"], p["dn_out_norm"], batch, seq)
    x1, h2 = _oproj_call(att_o.reshape(batch * seq, ATT_W), dn_o.reshape(batch * seq, DN_W),
                         x2, mod3, p["norm2_g"], p["w_o"], seq)
    y = _ffn_call(h2, x1, mod3, p["ffn_w_up"], p["ffn_conv_w"], p["ffn_conv_b"], p["ffn_w_down"], seq)
    return y.reshape(batch, seq, d)


def kernel(x_prompt, x_sample, c_prompt, c_sample, ada_w, ada_b, norm1_g, norm2_g, w_in,
           att_q_norm, att_k_norm, att_rpb, dn_conv_w, dn_a_log, dn_dt_bias, dn_out_norm, w_o,
           ffn_w_up, ffn_conv_w, ffn_conv_b, ffn_w_down):
    depth = ada_w.shape[0]
    bp, bs = c_prompt.shape[0], c_sample.shape[0]
    n_c = -(-(bp + bs) // 8) * 8
    gate_lanes = lambda a: jnp.pad(a.reshape(1, N_DIR * N_DN_HEADS),
                                   ((0, 0), (N_DIR * N_DN_HEADS, GATE_PAD - N_GATE)))
    blk = np.arange(LANES) // ATT_HD
    ones_bd = jnp.asarray(blk[:, None] == blk[None, :], BF16)
    xs = [x_prompt, x_sample]
    for l in range(depth):
        p = {
            "norm1_g": norm1_g[l].reshape(1, -1), "norm2_g": norm2_g[l].reshape(1, -1),
            "w_in": jnp.pad(w_in[l], ((0, 0), (0, GATE_PAD - N_GATE))).astype(BF16),
            "alog_l": gate_lanes(dn_a_log[l]), "dtb_l": gate_lanes(dn_dt_bias[l]),
            "bias_tbl": _natten_bias_table(att_rpb[l]),
            "qg2": jnp.tile(att_q_norm[l], 2).reshape(1, LANES),
            "kg2": jnp.tile(att_k_norm[l], 2).reshape(1, LANES),
            "ones_bd": ones_bd,
            "dn_conv_w": dn_conv_w[l], "dn_out_norm": dn_out_norm[l].reshape(1, -1),
            "w_o": w_o[l].astype(BF16), "ffn_w_up": ffn_w_up[l].astype(BF16),
            "ffn_conv_w": ffn_conv_w[l], "ffn_conv_b": ffn_conv_b[l].reshape(1, -1),
            "ffn_w_down": ffn_w_down[l].astype(BF16),
        }
        c_all = jnp.pad(jnp.concatenate([c_prompt, c_sample], axis=0), ((0, n_c - bp - bs), (0, 0)))
        mod = _mod_call(c_all, ada_w[l], ada_b[l])
        xs = [_trunk(xs[0], mod[:bp], p), _trunk(xs[1], mod[bp:bp + bs], p)]
    return tuple(xs)
```

```python
import functools

import numpy as np
import jax
import jax.numpy as jnp
from jax import lax
from jax.experimental import pallas as pl
from jax.experimental.pallas import tpu as pltpu

F32 = jnp.float32
BF16 = jnp.bfloat16

D_MODEL = 1024
GRID_W = 64
N_ATT_HEADS = 8
ATT_HD = 64
ATT_W = N_ATT_HEADS * ATT_HD
WIN_R = 8
WIN_C = 16
N_DN_HEADS = 4
DN_HD = 128
DN_W = N_DN_HEADS * DN_HD
N_DIR = 2
CHUNK = 64
D_FF = 2816
NORM_EPS = 1e-6
N_GATE = 2 * N_DIR * N_DN_HEADS
LANES = 128
GATE_PAD = LANES
IN_COLS_PAD = 3 * ATT_W + 4 * DN_W + GATE_PAD
MASK_NEG = -1e30

VMEM_LIMIT = 56 * 1024 * 1024
TOKEN_TILE = 512
FFN_CHUNK = 256
HALO = 16
IN_CHUNK = 512
assert ATT_W == IN_CHUNK and DN_W == IN_CHUNK
AQ_ROWS = DN_HD + CHUNK
NATTEN_UNROLL = 8
GDN_UNROLL = 16


def _cparams(n_axes):
    return pltpu.CompilerParams(dimension_semantics=("arbitrary",) * n_axes,
                                vmem_limit_bytes=VMEM_LIMIT)


def _const_spec(shape):
    nd = len(shape)
    return pl.BlockSpec(shape, lambda *_: (0,) * nd, pipeline_mode=pl.Buffered(1))


def _silu(x):
    return x * jax.nn.sigmoid(x)


def _dot(a, b):
    return jnp.dot(a, b, preferred_element_type=F32)


def _dot_nt(a, b):
    return lax.dot_general(a, b, (((1,), (1,)), ((), ())), preferred_element_type=F32)


def _dot_tn(a, b):
    return lax.dot_general(a, b, (((0,), (0,)), ((), ())), preferred_element_type=F32)


def _split2(x):
    hi = x.astype(BF16)
    lo = (x - hi.astype(F32)).astype(BF16)
    return hi, lo


def _split3(x):
    hi = x.astype(BF16)
    r = x - hi.astype(F32)
    mid = r.astype(BF16)
    lo = (r - mid.astype(F32)).astype(BF16)
    return hi, mid, lo


def _mod_kernel(c_ref, w_ref, b_ref, o_ref):
    s = _silu(c_ref[...])
    o_ref[...] = jnp.dot(s, w_ref[...], preferred_element_type=F32,
                         precision=lax.Precision.HIGHEST) + b_ref[...]


def _mod_call(c, ada_w, ada_b):
    bp, d = c.shape
    n = ada_w.shape[1]
    bn = 1024
    return pl.pallas_call(
        _mod_kernel,
        grid=(n // bn,),
        in_specs=[pl.BlockSpec((bp, d), lambda j: (0, 0)),
                  pl.BlockSpec((d, bn), lambda j: (0, j)),
                  pl.BlockSpec((1, bn), lambda j: (0, j))],
        out_specs=pl.BlockSpec((bp, bn), lambda j: (0, j)),
        out_shape=jax.ShapeDtypeStruct((bp, n), F32),
        compiler_params=_cparams(1),
        name="mod",
    )(c, ada_w, ada_b.reshape(1, n))


def _inproj_kernel(x_ref, xp_ref, xn_ref, mod_ref, g_ref, w_ref, cw_ref, qg_ref, kg_ref,
                   ones_att_ref, ones_dn_ref, alog_ref, dtb_ref,
                   att_ref, dn_ref, z_ref, gate_ref, *, per):
    i = pl.program_id(0)
    tm = x_ref.shape[0]
    cw = IN_CHUNK

    def norm_mod(x):
        ms = jnp.mean(x * x, axis=-1, keepdims=True)
        y = x * lax.rsqrt(ms + NORM_EPS) * g_ref[...]
        return (y * (1.0 + mod_ref[0, 1:2, :]) + mod_ref[0, 0:1, :]).astype(BF16)

    zero = jnp.zeros((HALO, D_MODEL), BF16)
    hp = jnp.where(i % per != 0, norm_mod(xp_ref[...]), zero)
    hn = jnp.where(i % per != per - 1, norm_mod(xn_ref[...]), zero)
    hx = jnp.concatenate([hp, norm_mod(x_ref[...]), hn], axis=0)
    h = hx[HALO:HALO + tm]

    def group_scale(y, ones_ref, mean_div):
        outs = []
        for b in range(cw // LANES):
            yb = y[:, b * LANES:(b + 1) * LANES]
            ss = _dot((yb * yb).astype(BF16), ones_ref[...])
            outs.append(lax.rsqrt(ss * (1.0 / mean_div) + NORM_EPS))
        return jnp.concatenate(outs, axis=1)

    def att_qk(gain_ref, scale):
        def fin(u, o):
            r = u * group_scale(u, ones_att_ref, ATT_HD) * (gain_ref[...] * scale)
            att_ref[:, o:o + cw] = r.astype(BF16)
        return fin

    def att_v(u, o):
        att_ref[:, o:o + cw] = u.astype(BF16)

    def dn_conv(kind):
        def fin(u, o):
            wc = cw_ref[:, o:o + cw]
            prev = pltpu.roll(u, 1, 0)[HALO:HALO + tm]
            nxt = pltpu.roll(u, tm + 2 * HALO - 1, 0)[HALO:HALO + tm]
            y = _silu(prev * wc[0:1, :] + u[HALO:HALO + tm] * wc[1:2, :] + nxt * wc[2:3, :])
            if kind == "q":
                y = y * group_scale(y, ones_dn_ref, 1.0) * (DN_HD ** -0.5)
            elif kind == "k":
                y = y * group_scale(y, ones_dn_ref, 1.0)
            dn_ref[:, o:o + cw] = y.astype(BF16)
        return fin

    def z_out(u, o):
        z_ref[...] = u.astype(BF16)

    def gates(u, o):
        lane = lax.broadcasted_iota(jnp.int32, (1, GATE_PAD), 1)
        t = u + dtb_ref[...]
        softplus = jnp.maximum(t, 0.0) + jnp.log1p(jnp.exp(-jnp.abs(t)))
        decay = -jnp.exp(alog_ref[...]) * softplus
        gate_ref[...] = jnp.where(lane < N_DIR * N_DN_HEADS, jax.nn.sigmoid(u), decay)

    o_dn, o_z, o_g = 3 * ATT_W, 3 * ATT_W + 3 * DN_W, 3 * ATT_W + 4 * DN_W
    jobs = [(0, cw, h, 0, att_qk(qg_ref, ATT_HD ** -0.5)),
            (cw, cw, h, cw, att_qk(kg_ref, 1.0)),
            (2 * cw, cw, h, 2 * cw, att_v),
            (o_dn, cw, hx, 0, dn_conv("q")),
            (o_dn + cw, cw, hx, cw, dn_conv("k")),
            (o_dn + 2 * cw, cw, hx, 2 * cw, dn_conv("v")),
            (o_z, DN_W, h, 0, z_out),
            (o_g, GATE_PAD, h, 0, gates)]
    proj = lambda job: _dot(job[2], w_ref[:, job[0]:job[0] + job[1]])
    u_next = proj(jobs[0])
    for n, job in enumerate(jobs):
        u = u_next
        if n + 1 < len(jobs):
            u_next = proj(jobs[n + 1])
        job[4](u, job[3])


def _inproj_call(x2, mod3, g1, w_cat, conv_w, qg, kg, ones_att, ones_dn, alog_l, dtb_l, seq):
    n_tok = x2.shape[0]
    tm = TOKEN_TILE
    per = seq // tm
    hb = tm // HALO
    last = n_tok // HALO - 1
    tok = lambda i: (i, 0)
    return pl.pallas_call(
        functools.partial(_inproj_kernel, per=per),
        grid=(n_tok // tm,),
        in_specs=[pl.BlockSpec((tm, D_MODEL), tok),
                  pl.BlockSpec((HALO, D_MODEL), lambda i: (jnp.maximum(i * hb - 1, 0), 0)),
                  pl.BlockSpec((HALO, D_MODEL), lambda i: (jnp.minimum((i + 1) * hb, last), 0)),
                  pl.BlockSpec((1, 6, D_MODEL), lambda i: (i // per, 0, 0)),
                  _const_spec((1, D_MODEL)),
                  _const_spec((D_MODEL, IN_COLS_PAD)),
                  _const_spec((3, 3 * DN_W)),
                  _const_spec((1, IN_CHUNK)),
                  _const_spec((1, IN_CHUNK)),
                  _const_spec((LANES, LANES)),
                  _const_spec((LANES, LANES)),
                  _const_spec((1, GATE_PAD)),
                  _const_spec((1, GATE_PAD))],
        out_specs=[pl.BlockSpec((tm, 3 * ATT_W), tok),
                   pl.BlockSpec((tm, 3 * DN_W), tok),
                   pl.BlockSpec((tm, DN_W), tok),
                   pl.BlockSpec((tm, GATE_PAD), tok)],
        out_shape=[jax.ShapeDtypeStruct((n_tok, 3 * ATT_W), BF16),
                   jax.ShapeDtypeStruct((n_tok, 3 * DN_W), BF16),
                   jax.ShapeDtypeStruct((n_tok, DN_W), BF16),
                   jax.ShapeDtypeStruct((n_tok, GATE_PAD), F32)],
        compiler_params=_cparams(1),
        name="inproj",
    )(x2, x2, x2, mod3, g1, w_cat, conv_w, qg, kg, ones_att, ones_dn, alog_l, dtb_l)


def _natten_kernel(q_ref, k_ref, v_ref, bias_ref, o_ref, *, rows):
    lane = lax.broadcasted_iota(jnp.int32, (1, LANES), 1)
    first = lane < ATT_HD
    band = WIN_R * GRID_W

    def row_group(i, carry):
        units = []
        for j in range(NATTEN_UNROLL):
            r = i * NATTEN_UNROLL + j
            r0 = jnp.clip(r - WIN_R // 2, 0, rows - WIN_R)
            qs = pl.ds(pl.multiple_of(r * GRID_W, GRID_W), GRID_W)
            ks = pl.ds(pl.multiple_of(r0 * GRID_W, GRID_W), band)
            units.append((qs, ks, r - r0))
        scores = []
        for qs, ks, dd in units:
            q = q_ref[0, qs, :]
            q2 = jnp.concatenate([jnp.where(first, q, 0), jnp.where(first, 0, q)], axis=0)
            bias = bias_ref[:, dd].reshape(2 * GRID_W, band)
            scores.append(_dot_nt(q2, k_ref[0, ks, :]) + bias)
        probs = []
        for s in scores:
            p = jnp.exp(s - jnp.max(s, axis=-1, keepdims=True))
            probs.append((p.astype(BF16), jnp.sum(p, axis=-1, keepdims=True)))
        for (qs, ks, _), (p, l) in zip(units, probs):
            o2 = _dot(p, v_ref[0, ks, :]) / l
            o_ref[0, qs, :] = jnp.where(first, o2[:GRID_W], o2[GRID_W:]).astype(BF16)
        return carry

    lax.fori_loop(0, rows // NATTEN_UNROLL, row_group, 0)


def _natten_call(att, bias_tbl, batch, seq):
    rows = seq // GRID_W
    assert rows >= WIN_R and rows % NATTEN_UNROLL == 0
    att3 = att.reshape(batch, seq, 3 * ATT_W)
    n_pair = N_ATT_HEADS // 2
    blk = (1, seq, LANES)
    return pl.pallas_call(
        functools.partial(_natten_kernel, rows=rows),
        grid=(n_pair, batch),
        in_specs=[pl.BlockSpec(blk, lambda p, b: (b, 0, p)),
                  pl.BlockSpec(blk, lambda p, b: (b, 0, n_pair + p)),
                  pl.BlockSpec(blk, lambda p, b: (b, 0, 2 * n_pair + p)),
                  pl.BlockSpec((2, WIN_R, GRID_W, WIN_R * GRID_W), lambda p, b: (p, 0, 0, 0))],
        out_specs=pl.BlockSpec(blk, lambda p, b: (b, 0, p)),
        out_shape=jax.ShapeDtypeStruct((batch, seq, ATT_W), BF16),
        compiler_params=_cparams(2),
        name="natten",
    )(att3, att3, att3, bias_tbl)


def _natten_bias_table(rpb):
    n_row = 2 * WIN_R - 1
    per = 2 * GRID_W
    v = jnp.zeros((N_ATT_HEADS, n_row, per), F32)
    v = v.at[:, :, :WIN_C].set(rpb[:, :, WIN_C - 1:])
    v = v.at[:, :, per - (WIN_C - 1):].set(rpb[:, :, :WIN_C - 1])
    toep = jnp.tile(v, (1, 1, GRID_W))[:, :, :GRID_W * (per - 1)]
    toep = toep.reshape(N_ATT_HEADS, n_row, GRID_W, per - 1)[:, :, :, :GRID_W]
    qc = np.arange(GRID_W)[:, None]
    kc = np.arange(GRID_W)[None, :]
    c0 = np.clip(qc - WIN_C // 2, 0, GRID_W - WIN_C)
    valid = (kc >= c0) & (kc < c0 + WIN_C)
    toep = jnp.where(valid[None, None], toep, MASK_NEG)
    bands = [jnp.swapaxes(toep[:, WIN_R - 1 - dd:2 * WIN_R - 1 - dd], 1, 2) for dd in range(WIN_R)]
    tbl = jnp.stack(bands, axis=1)
    return tbl.reshape(N_ATT_HEADS, WIN_R, GRID_W, WIN_R * GRID_W)


def _gdn_kernel(q_ref, k_ref, v_ref, gate_ref, o_ref, aq_s, b_s, eg_s, *, seq):
    n_chunk = seq // CHUNK
    head = pl.program_id(1)
    C = CHUNK
    lane = lax.broadcasted_iota(jnp.int32, (1, LANES), 1)
    left = lane < C
    ri = lax.broadcasted_iota(jnp.int32, (C, LANES), 0)
    li = lax.broadcasted_iota(jnp.int32, (C, LANES), 1)
    ci = li & (C - 1)
    fwd_half = li < C
    lag = jnp.where(fwd_half, ri - ci, ci - ri)
    causal_p = lag >= 0
    strict_p = lag > 0
    eye_p = (lag == 0).astype(F32)
    eye_left = (li == ri)
    b16 = ((ci >> 4) == (ri >> 4)).astype(F32)
    b32 = ((ci >> 5) == (ri >> 5)).astype(F32)
    m16 = b16
    m32 = b32 - b16
    m64 = 1.0 - b32

    def blockdiag(p16):
        return jnp.concatenate([jnp.where(left, p16, 0), jnp.where(left, 0, p16)], axis=0)

    def pmm(a_list, b_list):
        return [_dot(a.astype(BF16), blockdiag(b.astype(BF16))) for a, b in zip(a_list, b_list)]

    def lane_pick(x, col):
        return jnp.sum(jnp.where(lane == col, x, 0.0), axis=-1, keepdims=True)

    def chunk_prep(c):
        base = pl.multiple_of(c * C, C)
        stores = []
        q16 = q_ref[0, pl.ds(base, C), :]
        k16 = k_ref[0, pl.ds(base, C), :]
        q, k, v = q16.astype(F32), k16.astype(F32), v_ref[0, pl.ds(base, C), :].astype(F32)
        gram = _dot_nt(jnp.concatenate([k16, q16], axis=0), k16)
        gkk = jnp.concatenate([gram[:C], gram[:C]], axis=1)
        gqk = jnp.concatenate([gram[C:], gram[C:]], axis=1)

        gat = gate_ref[0, pl.ds(base, C), :]
        beta, gcol, grow, glast = [], [], [], []
        for d in range(N_DIR):
            beta.append(lane_pick(gat, d * N_DN_HEADS + head))
            g = lane_pick(gat, N_DIR * N_DN_HEADS + d * N_DN_HEADS + head)
            gb = jnp.broadcast_to(g, (C, LANES))
            inc = (ri <= ci) if d == 0 else (ri >= ci)
            gr = jnp.sum(jnp.where(inc, gb, 0.0), axis=0, keepdims=True)
            grow.append(gr)
            gcol.append(jnp.sum(jnp.where(eye_left, jnp.broadcast_to(gr, (C, LANES)), 0.0),
                                axis=-1, keepdims=True))
            glast.append(jnp.sum(g, axis=0, keepdims=True))
        beta_p = jnp.where(fwd_half, beta[0], beta[1])
        diff = jnp.where(fwd_half, gcol[0] - grow[0], gcol[1] - grow[1])
        decay = jnp.exp(jnp.where(causal_p, diff, MASK_NEG))
        l_p = jnp.where(strict_p, beta_p * gkk * decay, 0.0)
        intra = jnp.where(causal_p, gqk * decay, 0.0).astype(BF16)

        rhs, kd, qd = [], [], []
        for d in range(N_DIR):
            eg = jnp.exp(gcol[d])
            kb = k * beta[d]
            rhs.append(jnp.concatenate([v * beta[d], kb * eg], axis=1))
            kd.append((k * jnp.exp(glast[d] - gcol[d])).astype(BF16))
            qd.append(q * eg)
            stores.append((eg_s, (d, pl.ds(pl.multiple_of(c * 8, 8), 8)),
                           jnp.broadcast_to(jnp.exp(glast[d]), (8, LANES))))
        return dict(l_p=l_p, rhs=jnp.concatenate(rhs, axis=0), intra=intra, kd=kd, qd=qd,
                    stores=stores, c=c)

    def phase_a_group(i, carry):
        preps = [chunk_prep(i * GDN_UNROLL + j) for j in range(GDN_UNROLL)]
        l_ps = [p["l_p"] for p in preps]
        add = lambda xs, ys: [x + y for x, y in zip(xs, ys)]
        n = [-l_p * m16 for l_p in l_ps]
        xp = n
        for _ in range(3):
            xp = pmm(xp, xp)
            n = add(add(n, xp), pmm(n, xp))
        for m in (m32, m64):
            cm = [l_p * m for l_p in l_ps]
            y = add(cm, pmm(n, cm))
            n = [a - b - c for a, b, c in zip(n, y, pmm(y, n))]

        uw = [(p["rhs"] + _dot(blockdiag(a.astype(BF16)), p["rhs"].astype(BF16))).astype(BF16)
              for p, a in zip(preps, n)]

        iw = [_dot(blockdiag(p["intra"]), x) for p, x in zip(preps, uw)]
        kt = [[_dot_tn(p["kd"][d], x[d * C:(d + 1) * C]) for d in range(N_DIR)]
              for p, x in zip(preps, uw)]
        for p, w_iw, w_kt in zip(preps, iw, kt):
            c = p["c"]
            for d in range(N_DIR):
                rows_aq = pl.multiple_of(c * AQ_ROWS, C)
                b_s[d, pl.ds(pl.multiple_of(c * DN_HD, DN_HD), DN_HD), :] = w_kt[d][:, :LANES]
                aq_s[d, pl.ds(rows_aq, DN_HD), :] = (-w_kt[d][:, LANES:]).astype(BF16)
                aq_s[d, pl.ds(rows_aq + DN_HD, C), :] = (
                    p["qd"][d] - w_iw[d * C:(d + 1) * C, LANES:]).astype(BF16)
            o_ref[0, pl.ds(pl.multiple_of(c * C, C), C), :] = w_iw[:C, :LANES] + w_iw[C:, :LANES]
            for ref, idx, val in p["stores"]:
                ref[idx + (slice(None),)] = val
        return carry

    lax.fori_loop(0, n_chunk // GDN_UNROLL, phase_a_group, 0)

    def phase_b(n, states):
        new_states = []
        for d in range(N_DIR):
            c = n if d == 0 else n_chunk - 1 - n
            s = states[d]
            r = _dot(aq_s[d, pl.ds(pl.multiple_of(c * AQ_ROWS, C), AQ_ROWS), :], s.astype(BF16))
            o_ref[0, pl.ds(pl.multiple_of(c * C, C), C), :] += r[DN_HD:]
            eg = eg_s[d, pl.ds(pl.multiple_of(c * 8, 8), 1), :]
            new_states.append(s * eg + r[:DN_HD]
                              + b_s[d, pl.ds(pl.multiple_of(c * DN_HD, DN_HD), DN_HD), :])
        return tuple(new_states)

    zero = jnp.zeros((DN_HD, DN_HD), F32)
    lax.fori_loop(0, n_chunk, phase_b, (zero, zero))


def _gdn_call(dn, gates, batch, seq):
    dn3 = dn.reshape(batch, seq, 3 * DN_W)
    g3 = gates.reshape(batch, seq, GATE_PAD)
    blk = (1, seq, LANES)
    nh = N_DN_HEADS
    n_chunk = seq // CHUNK
    assert seq % (CHUNK * GDN_UNROLL) == 0
    return pl.pallas_call(
        functools.partial(_gdn_kernel, seq=seq),
        grid=(batch, nh),
        in_specs=[pl.BlockSpec(blk, lambda b, h: (b, 0, h)),
                  pl.BlockSpec(blk, lambda b, h: (b, 0, nh + h)),
                  pl.BlockSpec(blk, lambda b, h: (b, 0, 2 * nh + h)),
                  pl.BlockSpec(blk, lambda b, h: (b, 0, 0))],
        out_specs=pl.BlockSpec(blk, lambda b, h: (b, 0, h)),
        out_shape=jax.ShapeDtypeStruct((batch, seq, DN_W), F32),
        scratch_shapes=[
            pltpu.VMEM((N_DIR, AQ_ROWS * n_chunk, LANES), BF16),
            pltpu.VMEM((N_DIR, DN_HD * n_chunk, LANES), F32),
            pltpu.VMEM((N_DIR, 8 * n_chunk, LANES), F32)],
        compiler_params=_cparams(2),
        name="gdn",
    )(dn3, dn3, dn3, g3)


def _oproj_kernel(att_ref, dn_ref, z_ref, gn_ref, x_ref, mod_ref, g_ref, w_ref, x1_ref, h_ref):
    heads = []
    for hd in range(N_DN_HEADS):
        cols = slice(hd * DN_HD, (hd + 1) * DN_HD)
        o = dn_ref[:, cols]
        o = o * lax.rsqrt(jnp.mean(o * o, axis=-1, keepdims=True) + NORM_EPS) * gn_ref[...]
        heads.append((o * _silu(z_ref[:, cols].astype(F32))).astype(BF16))
    dn = jnp.concatenate(heads, axis=1)
    mix = _dot(att_ref[...], w_ref[0:ATT_W, :]) + _dot(dn, w_ref[ATT_W:, :])
    x1 = x_ref[...] + mod_ref[0, 2:3, :] * mix
    x1_ref[...] = x1
    ms = jnp.mean(x1 * x1, axis=-1, keepdims=True)
    y = x1 * lax.rsqrt(ms + NORM_EPS) * g_ref[...]
    h_ref[...] = (y * (1.0 + mod_ref[0, 4:5, :]) + mod_ref[0, 3:4, :]).astype(BF16)


def _oproj_call(att, dno, z, gn, x2, mod3, g2, w_o, seq):
    n_tok = x2.shape[0]
    tm = TOKEN_TILE
    per = seq // tm
    tok = lambda i: (i, 0)
    return pl.pallas_call(
        _oproj_kernel,
        grid=(n_tok // tm,),
        in_specs=[pl.BlockSpec((tm, ATT_W), tok),
                  pl.BlockSpec((tm, DN_W), tok),
                  pl.BlockSpec((tm, DN_W), tok),
                  _const_spec((1, DN_HD)),
                  pl.BlockSpec((tm, D_MODEL), tok),
                  pl.BlockSpec((1, 6, D_MODEL), lambda i: (i // per, 0, 0)),
                  _const_spec((1, D_MODEL)),
                  _const_spec((D_MODEL, D_MODEL))],
        out_specs=[pl.BlockSpec((tm, D_MODEL), tok),
                   pl.BlockSpec((tm, D_MODEL), tok)],
        out_shape=[jax.ShapeDtypeStruct((n_tok, D_MODEL), F32),
                   jax.ShapeDtypeStruct((n_tok, D_MODEL), BF16)],
        compiler_params=_cparams(1),
        name="oproj",
    )(att, dno, z, gn, x2, mod3, g2, w_o)


def _ffn_kernel(h_ref, hp_ref, hn_ref, x1_ref, mod_ref, wu_ref, cw_ref, cb_ref, wd_ref, y_ref,
                u_s, act_s, *, per):
    i = pl.program_id(0)
    tm = h_ref.shape[0]
    zero = jnp.zeros((HALO, D_MODEL), BF16)
    hp = jnp.where(i % per != 0, hp_ref[...], zero)
    hn = jnp.where(i % per != per - 1, hn_ref[...], zero)
    hx = jnp.concatenate([hp, h_ref[...], hn], axis=0)
    n_chunk = D_FF // FFN_CHUNK

    def up(j):
        for half, o in enumerate((j * FFN_CHUNK, D_FF + j * FFN_CHUNK)):
            u_s[j % 2, half] = _dot(hx, wu_ref[:, o:o + FFN_CHUNK])

    def gate(j):
        halves = []
        for half, o in enumerate((j * FFN_CHUNK, D_FF + j * FFN_CHUNK)):
            cw = cw_ref[:, o:o + FFN_CHUNK]
            taps = [u_s[j % 2, half, HALO - 1 + k:HALO - 1 + k + tm, :] for k in range(3)]
            halves.append(taps[0] * cw[0:1, :] + taps[1] * cw[1:2, :] + taps[2] * cw[2:3, :]
                          + cb_ref[:, o:o + FFN_CHUNK])
        act_s[:, j * FFN_CHUNK:(j + 1) * FFN_CHUNK] = (_silu(halves[0]) * halves[1]).astype(BF16)

    up(0)
    for j in range(n_chunk):
        if j + 1 < n_chunk:
            up(j + 1)
        gate(j)
    y_ref[...] = x1_ref[...] + mod_ref[0, 5:6, :] * _dot(act_s[...], wd_ref[...])


def _ffn_call(h2, x1, mod3, w_up, conv_w, conv_b, w_down, seq):
    n_tok = x1.shape[0]
    tm = TOKEN_TILE
    per = seq // tm
    hb = tm // HALO
    last = n_tok // HALO - 1
    tok = lambda i: (i, 0)
    return pl.pallas_call(
        functools.partial(_ffn_kernel, per=per),
        grid=(n_tok // tm,),
        in_specs=[pl.BlockSpec((tm, D_MODEL), tok),
                  pl.BlockSpec((HALO, D_MODEL), lambda i: (jnp.maximum(i * hb - 1, 0), 0)),
                  pl.BlockSpec((HALO, D_MODEL), lambda i: (jnp.minimum((i + 1) * hb, last), 0)),
                  pl.BlockSpec((tm, D_MODEL), tok),
                  pl.BlockSpec((1, 6, D_MODEL), lambda i: (i // per, 0, 0)),
                  _const_spec((D_MODEL, 2 * D_FF)),
                  _const_spec((3, 2 * D_FF)),
                  _const_spec((1, 2 * D_FF)),
                  _const_spec((D_FF, D_MODEL))],
        out_specs=pl.BlockSpec((tm, D_MODEL), tok),
        out_shape=jax.ShapeDtypeStruct((n_tok, D_MODEL), F32),
        scratch_shapes=[pltpu.VMEM((2, 2, tm + 2 * HALO, FFN_CHUNK), F32),
                        pltpu.VMEM((tm, D_FF), BF16)],
        compiler_params=_cparams(1),
        name="ffn",
    )(h2, h2, h2, x1, mod3, w_up, conv_w, conv_b, w_down)


def _trunk(x, mod, p):
    batch, seq, d = x.shape
    assert seq % TOKEN_TILE == 0 and seq % GRID_W == 0
    x2 = x.reshape(batch * seq, d)
    mod3 = mod.reshape(batch, 6, d)
    att, dn, z, gates = _inproj_call(x2, mod3, p["norm1_g"], p["w_in"], p["dn_conv_w"], p["qg"], p["kg"],
                                     p["ones_att"], p["ones_dn"], p["alog_l"], p["dtb_l"], seq)
    att_o = _natten_call(att, p["bias_tbl"], batch, seq)
    dn_o = _gdn_call(dn, gates, batch, seq)
    x1, h2 = _oproj_call(att_o.reshape(batch * seq, ATT_W), dn_o.reshape(batch * seq, DN_W), z,
                         p["dn_out_norm"], x2, mod3, p["norm2_g"], p["w_o"], seq)
    y = _ffn_call(h2, x1, mod3, p["ffn_w_up"], p["ffn_conv_w"], p["ffn_conv_b"], p["ffn_w_down"], seq)
    return y.reshape(batch, seq, d)


def kernel(x_prompt, x_sample, c_prompt, c_sample, ada_w, ada_b, norm1_g, norm2_g, w_in,
           att_q_norm, att_k_norm, att_rpb, dn_conv_w, dn_a_log, dn_dt_bias, dn_out_norm, w_o,
           ffn_w_up, ffn_conv_w, ffn_conv_b, ffn_w_down):
    depth = ada_w.shape[0]
    bp, bs = c_prompt.shape[0], c_sample.shape[0]
    n_c = -(-(bp + bs) // 8) * 8
    gate_lanes = lambda a: jnp.pad(a.reshape(1, N_DIR * N_DN_HEADS),
                                   ((0, 0), (N_DIR * N_DN_HEADS, GATE_PAD - N_GATE)))
    blk = np.arange(LANES) // ATT_HD
    ones_bd = jnp.asarray(blk[:, None] == blk[None, :], BF16)
    xs = [x_prompt, x_sample]
    for l in range(depth):
        p = {
            "norm1_g": norm1_g[l].reshape(1, -1), "norm2_g": norm2_g[l].reshape(1, -1),
            "w_in": jnp.pad(w_in[l], ((0, 0), (0, GATE_PAD - N_GATE))).astype(BF16),
            "alog_l": gate_lanes(dn_a_log[l]), "dtb_l": gate_lanes(dn_dt_bias[l]),
            "bias_tbl": _natten_bias_table(att_rpb[l]),
            "qg": jnp.tile(att_q_norm[l], N_ATT_HEADS).reshape(1, ATT_W),
            "kg": jnp.tile(att_k_norm[l], N_ATT_HEADS).reshape(1, ATT_W),
            "ones_att": ones_bd, "ones_dn": jnp.ones((LANES, LANES), BF16),
            "dn_conv_w": dn_conv_w[l], "dn_out_norm": dn_out_norm[l].reshape(1, -1),
            "w_o": w_o[l].astype(BF16), "ffn_w_up": ffn_w_up[l].astype(BF16),
            "ffn_conv_w": ffn_conv_w[l], "ffn_conv_b": ffn_conv_b[l].reshape(1, -1),
            "ffn_w_down": ffn_w_down[l].astype(BF16),
        }
        c_all = jnp.pad(jnp.concatenate([c_prompt, c_sample], axis=0), ((0, n_c - bp - bs), (0, 0)))
        mod = _mod_call(c_all, ada_w[l], ada_b[l])
        xs = [_trunk(xs[0], mod[:bp], p), _trunk(xs[1], mod[bp:bp + bs], p)]
    return tuple(xs)
```

```python
import functools

import numpy as np
import jax
import jax.numpy as jnp
from jax import lax
from jax.experimental import pallas as pl
from jax.experimental.pallas import tpu as pltpu

F32 = jnp.float32
BF16 = jnp.bfloat16

D_MODEL = 1024
GRID_W = 64
N_ATT_HEADS = 8
ATT_HD = 64
ATT_W = N_ATT_HEADS * ATT_HD
WIN_R = 8
WIN_C = 16
N_DN_HEADS = 4
DN_HD = 128
DN_W = N_DN_HEADS * DN_HD
N_DIR = 2
CHUNK = 64
D_FF = 2816
NORM_EPS = 1e-6
N_GATE = 2 * N_DIR * N_DN_HEADS
LANES = 128
GATE_PAD = LANES
IN_COLS_PAD = 3 * ATT_W + 4 * DN_W + GATE_PAD
MASK_NEG = -1e30

VMEM_LIMIT = 56 * 1024 * 1024
TOKEN_TILE = 512
FFN_CHUNK = 256
FFN_DOWN_GROUP = 4
HALO = 16
IN_CHUNK = 512
assert ATT_W == IN_CHUNK and DN_W == IN_CHUNK
AQ_ROWS = DN_HD + CHUNK
NATTEN_UNROLL = 8
GDN_UNROLL = 16
GDN_STAGES = 15


def _cparams(n_axes):
    return pltpu.CompilerParams(dimension_semantics=("arbitrary",) * n_axes,
                                vmem_limit_bytes=VMEM_LIMIT)


def _const_spec(shape):
    nd = len(shape)
    return pl.BlockSpec(shape, lambda *_: (0,) * nd, pipeline_mode=pl.Buffered(1))


def _silu(x):
    return x * jax.nn.sigmoid(x)


def _dot(a, b):
    return jnp.dot(a, b, preferred_element_type=F32)


def _dot_nt(a, b):
    return lax.dot_general(a, b, (((1,), (1,)), ((), ())), preferred_element_type=F32)


def _dot_tn(a, b):
    return lax.dot_general(a, b, (((0,), (0,)), ((), ())), preferred_element_type=F32)


def _split2(x):
    hi = x.astype(BF16)
    lo = (x - hi.astype(F32)).astype(BF16)
    return hi, lo


def _split3(x):
    hi = x.astype(BF16)
    r = x - hi.astype(F32)
    mid = r.astype(BF16)
    lo = (r - mid.astype(F32)).astype(BF16)
    return hi, mid, lo


def _mod_kernel(c_ref, w_ref, b_ref, o_ref):
    s = _silu(c_ref[...])
    o_ref[...] = jnp.dot(s, w_ref[...], preferred_element_type=F32,
                         precision=lax.Precision.HIGHEST) + b_ref[...]


def _mod_call(c, ada_w, ada_b):
    bp, d = c.shape
    n = ada_w.shape[1]
    bn = 1024
    return pl.pallas_call(
        _mod_kernel,
        grid=(n // bn,),
        in_specs=[pl.BlockSpec((bp, d), lambda j: (0, 0)),
                  pl.BlockSpec((d, bn), lambda j: (0, j)),
                  pl.BlockSpec((1, bn), lambda j: (0, j))],
        out_specs=pl.BlockSpec((bp, bn), lambda j: (0, j)),
        out_shape=jax.ShapeDtypeStruct((bp, n), F32),
        compiler_params=_cparams(1),
        name="mod",
    )(c, ada_w, ada_b.reshape(1, n))


def _inproj_kernel(x_ref, xp_ref, xn_ref, mod_ref, g_ref, w_ref, cw_ref, qg_ref, kg_ref,
                   ones_att_ref, ones_dn_ref, alog_ref, dtb_ref,
                   att_ref, dn_ref, z_ref, gate_ref, *, per):
    i = pl.program_id(0)
    tm = x_ref.shape[0]
    cw = IN_CHUNK

    def norm_mod(x):
        ms = jnp.mean(x * x, axis=-1, keepdims=True)
        y = x * lax.rsqrt(ms + NORM_EPS) * g_ref[...]
        return (y * (1.0 + mod_ref[0, 1:2, :]) + mod_ref[0, 0:1, :]).astype(BF16)

    zero = jnp.zeros((HALO, D_MODEL), BF16)
    hp = jnp.where(i % per != 0, norm_mod(xp_ref[...]), zero)
    hn = jnp.where(i % per != per - 1, norm_mod(xn_ref[...]), zero)
    hx = jnp.concatenate([hp, norm_mod(x_ref[...]), hn], axis=0)
    h = hx[HALO:HALO + tm]

    def group_scale(y, ones_ref, mean_div):
        outs = []
        for b in range(cw // LANES):
            yb = y[:, b * LANES:(b + 1) * LANES]
            ss = _dot((yb * yb).astype(BF16), ones_ref[...])
            outs.append(lax.rsqrt(ss * (1.0 / mean_div) + NORM_EPS))
        return jnp.concatenate(outs, axis=1)

    def att_qk(gain_ref, scale):
        def fin(u, o):
            r = u * group_scale(u, ones_att_ref, ATT_HD) * (gain_ref[...] * scale)
            att_ref[:, o:o + cw] = r.astype(BF16)
        return fin

    def att_v(u, o):
        att_ref[:, o:o + cw] = u.astype(BF16)

    def dn_conv(kind):
        def fin(u, o):
            wc = cw_ref[:, o:o + cw]
            prev = pltpu.roll(u, 1, 0)[HALO:HALO + tm]
            nxt = pltpu.roll(u, tm + 2 * HALO - 1, 0)[HALO:HALO + tm]
            y = _silu(prev * wc[0:1, :] + u[HALO:HALO + tm] * wc[1:2, :] + nxt * wc[2:3, :])
            if kind == "q":
                y = y * group_scale(y, ones_dn_ref, 1.0) * (DN_HD ** -0.5)
            elif kind == "k":
                y = y * group_scale(y, ones_dn_ref, 1.0)
            dn_ref[:, o:o + cw] = y.astype(BF16)
        return fin

    def z_out(u, o):
        z_ref[...] = u.astype(BF16)

    def gates(u, o):
        lane = lax.broadcasted_iota(jnp.int32, (1, GATE_PAD), 1)
        t = u + dtb_ref[...]
        softplus = jnp.maximum(t, 0.0) + jnp.log1p(jnp.exp(-jnp.abs(t)))
        decay = -jnp.exp(alog_ref[...]) * softplus
        gate_ref[...] = jnp.where(lane < N_DIR * N_DN_HEADS, jax.nn.sigmoid(u), decay)

    o_dn, o_z, o_g = 3 * ATT_W, 3 * ATT_W + 3 * DN_W, 3 * ATT_W + 4 * DN_W
    jobs = [(0, cw, h, 0, att_qk(qg_ref, ATT_HD ** -0.5)),
            (cw, cw, h, cw, att_qk(kg_ref, 1.0)),
            (2 * cw, cw, h, 2 * cw, att_v),
            (o_dn, cw, hx, 0, dn_conv("q")),
            (o_dn + cw, cw, hx, cw, dn_conv("k")),
            (o_dn + 2 * cw, cw, hx, 2 * cw, dn_conv("v")),
            (o_z, DN_W, h, 0, z_out),
            (o_g, GATE_PAD, h, 0, gates)]
    proj = lambda job: _dot(job[2], w_ref[:, job[0]:job[0] + job[1]])
    u_next = proj(jobs[0])
    for n, job in enumerate(jobs):
        u = u_next
        if n + 1 < len(jobs):
            u_next = proj(jobs[n + 1])
        job[4](u, job[3])


def _inproj_call(x2, mod3, g1, w_cat, conv_w, qg, kg, ones_att, ones_dn, alog_l, dtb_l, seq):
    n_tok = x2.shape[0]
    tm = TOKEN_TILE
    per = seq // tm
    hb = tm // HALO
    last = n_tok // HALO - 1
    tok = lambda i: (i, 0)
    return pl.pallas_call(
        functools.partial(_inproj_kernel, per=per),
        grid=(n_tok // tm,),
        in_specs=[pl.BlockSpec((tm, D_MODEL), tok),
                  pl.BlockSpec((HALO, D_MODEL), lambda i: (jnp.maximum(i * hb - 1, 0), 0)),
                  pl.BlockSpec((HALO, D_MODEL), lambda i: (jnp.minimum((i + 1) * hb, last), 0)),
                  pl.BlockSpec((1, 6, D_MODEL), lambda i: (i // per, 0, 0)),
                  _const_spec((1, D_MODEL)),
                  _const_spec((D_MODEL, IN_COLS_PAD)),
                  _const_spec((3, 3 * DN_W)),
                  _const_spec((1, IN_CHUNK)),
                  _const_spec((1, IN_CHUNK)),
                  _const_spec((LANES, LANES)),
                  _const_spec((LANES, LANES)),
                  _const_spec((1, GATE_PAD)),
                  _const_spec((1, GATE_PAD))],
        out_specs=[pl.BlockSpec((tm, 3 * ATT_W), tok),
                   pl.BlockSpec((tm, 3 * DN_W), tok),
                   pl.BlockSpec((tm, DN_W), tok),
                   pl.BlockSpec((tm, GATE_PAD), tok)],
        out_shape=[jax.ShapeDtypeStruct((n_tok, 3 * ATT_W), BF16),
                   jax.ShapeDtypeStruct((n_tok, 3 * DN_W), BF16),
                   jax.ShapeDtypeStruct((n_tok, DN_W), BF16),
                   jax.ShapeDtypeStruct((n_tok, GATE_PAD), F32)],
        compiler_params=_cparams(1),
        name="inproj",
    )(x2, x2, x2, mod3, g1, w_cat, conv_w, qg, kg, ones_att, ones_dn, alog_l, dtb_l)


def _natten_kernel(q_ref, k_ref, v_ref, bias_ref, o_ref, *, rows):
    lane = lax.broadcasted_iota(jnp.int32, (1, LANES), 1)
    first = lane < ATT_HD
    band = WIN_R * GRID_W

    def row_group(i, carry):
        units = []
        for j in range(NATTEN_UNROLL):
            r = i * NATTEN_UNROLL + j
            r0 = jnp.clip(r - WIN_R // 2, 0, rows - WIN_R)
            qs = pl.ds(pl.multiple_of(r * GRID_W, GRID_W), GRID_W)
            ks = pl.ds(pl.multiple_of(r0 * GRID_W, GRID_W), band)
            units.append((qs, ks, r - r0))
        scores = []
        for qs, ks, dd in units:
            q = q_ref[0, qs, :]
            q2 = jnp.concatenate([jnp.where(first, q, 0), jnp.where(first, 0, q)], axis=0)
            bias = bias_ref[:, dd].reshape(2 * GRID_W, band)
            scores.append(_dot_nt(q2, k_ref[0, ks, :]) + bias)
        probs = []
        for s in scores:
            p = jnp.exp(s - jnp.max(s, axis=-1, keepdims=True))
            probs.append((p.astype(BF16), jnp.sum(p, axis=-1, keepdims=True)))
        for (qs, ks, _), (p, l) in zip(units, probs):
            o2 = _dot(p, v_ref[0, ks, :]) / l
            o_ref[0, qs, :] = jnp.where(first, o2[:GRID_W], o2[GRID_W:]).astype(BF16)
        return carry

    lax.fori_loop(0, rows // NATTEN_UNROLL, row_group, 0)


def _natten_call(att, bias_tbl, batch, seq):
    rows = seq // GRID_W
    assert rows >= WIN_R and rows % NATTEN_UNROLL == 0
    att3 = att.reshape(batch, seq, 3 * ATT_W)
    n_pair = N_ATT_HEADS // 2
    blk = (1, seq, LANES)
    return pl.pallas_call(
        functools.partial(_natten_kernel, rows=rows),
        grid=(n_pair, batch),
        in_specs=[pl.BlockSpec(blk, lambda p, b: (b, 0, p)),
                  pl.BlockSpec(blk, lambda p, b: (b, 0, n_pair + p)),
                  pl.BlockSpec(blk, lambda p, b: (b, 0, 2 * n_pair + p)),
                  pl.BlockSpec((2, WIN_R, GRID_W, WIN_R * GRID_W), lambda p, b: (p, 0, 0, 0))],
        out_specs=pl.BlockSpec(blk, lambda p, b: (b, 0, p)),
        out_shape=jax.ShapeDtypeStruct((batch, seq, ATT_W), BF16),
        compiler_params=_cparams(2),
        name="natten",
    )(att3, att3, att3, bias_tbl)


def _natten_bias_table(rpb):
    n_row = 2 * WIN_R - 1
    per = 2 * GRID_W
    v = jnp.zeros((N_ATT_HEADS, n_row, per), F32)
    v = v.at[:, :, :WIN_C].set(rpb[:, :, WIN_C - 1:])
    v = v.at[:, :, per - (WIN_C - 1):].set(rpb[:, :, :WIN_C - 1])
    toep = jnp.tile(v, (1, 1, GRID_W))[:, :, :GRID_W * (per - 1)]
    toep = toep.reshape(N_ATT_HEADS, n_row, GRID_W, per - 1)[:, :, :, :GRID_W]
    qc = np.arange(GRID_W)[:, None]
    kc = np.arange(GRID_W)[None, :]
    c0 = np.clip(qc - WIN_C // 2, 0, GRID_W - WIN_C)
    valid = (kc >= c0) & (kc < c0 + WIN_C)
    toep = jnp.where(valid[None, None], toep, MASK_NEG)
    bands = [jnp.swapaxes(toep[:, WIN_R - 1 - dd:2 * WIN_R - 1 - dd], 1, 2) for dd in range(WIN_R)]
    tbl = jnp.stack(bands, axis=1)
    return tbl.reshape(N_ATT_HEADS, WIN_R, GRID_W, WIN_R * GRID_W)


def _gdn_kernel(q_ref, k_ref, v_ref, gate_ref, o_ref, aq_s, b_s, eg_s, o0_s, *, seq, n_units):
    s = pl.program_id(0)
    n_chunk = seq // CHUNK
    n_group = n_chunk // GDN_UNROLL
    head = jnp.minimum(s, n_units - 1) % N_DN_HEADS
    wslot = s % 2
    rslot = 1 - wslot
    C = CHUNK
    lane = lax.broadcasted_iota(jnp.int32, (1, LANES), 1)
    left = lane < C
    ri = lax.broadcasted_iota(jnp.int32, (C, LANES), 0)
    li = lax.broadcasted_iota(jnp.int32, (C, LANES), 1)
    ci = li & (C - 1)
    fwd_half = li < C
    lag = jnp.where(fwd_half, ri - ci, ci - ri)
    causal_p = lag >= 0
    strict_p = lag > 0
    eye_left = (li == ri)
    b16 = ((ci >> 4) == (ri >> 4)).astype(F32)
    b32 = ((ci >> 5) == (ri >> 5)).astype(F32)
    m16 = b16
    m32 = b32 - b16
    m64 = 1.0 - b32

    @pl.when(s == 0)
    def _():
        aq_s[1] = jnp.zeros(aq_s.shape[1:], aq_s.dtype)
        b_s[1] = jnp.zeros(b_s.shape[1:], b_s.dtype)
        eg_s[1] = jnp.zeros(eg_s.shape[1:], eg_s.dtype)
        o0_s[1] = jnp.zeros(o0_s.shape[1:], o0_s.dtype)

    def blockdiag(p16):
        return jnp.concatenate([jnp.where(left, p16, 0), jnp.where(left, 0, p16)], axis=0)

    def pmm(a_list, b_list):
        return [_dot(a.astype(BF16), blockdiag(b.astype(BF16))) for a, b in zip(a_list, b_list)]

    def lane_pick(x, col):
        return jnp.sum(jnp.where(lane == col, x, 0.0), axis=-1, keepdims=True)

    def chunk_prep(c):
        rows = slice(c * C, (c + 1) * C)
        q16 = q_ref[0, rows, :]
        k16 = k_ref[0, rows, :]
        q, k, v = q16.astype(F32), k16.astype(F32), v_ref[0, rows, :].astype(F32)
        gram = _dot_nt(jnp.concatenate([k16, q16], axis=0), k16)
        gkk = jnp.concatenate([gram[:C], gram[:C]], axis=1)
        gqk = jnp.concatenate([gram[C:], gram[C:]], axis=1)

        gat = gate_ref[0, rows, :]
        beta, gcol, grow, glast = [], [], [], []
        for d in range(N_DIR):
            beta.append(lane_pick(gat, d * N_DN_HEADS + head))
            g = lane_pick(gat, N_DIR * N_DN_HEADS + d * N_DN_HEADS + head)
            gb = jnp.broadcast_to(g, (C, LANES))
            inc = (ri <= ci) if d == 0 else (ri >= ci)
            gr = jnp.sum(jnp.where(inc, gb, 0.0), axis=0, keepdims=True)
            grow.append(gr)
            gcol.append(jnp.sum(jnp.where(eye_left, jnp.broadcast_to(gr, (C, LANES)), 0.0),
                                axis=-1, keepdims=True))
            glast.append(jnp.sum(g, axis=0, keepdims=True))
        beta_p = jnp.where(fwd_half, beta[0], beta[1])
        diff = jnp.where(fwd_half, gcol[0] - grow[0], gcol[1] - grow[1])
        decay = jnp.exp(jnp.where(causal_p, diff, MASK_NEG))
        l_p = jnp.where(strict_p, beta_p * gkk * decay, 0.0)
        intra = jnp.where(causal_p, gqk * decay, 0.0).astype(BF16)

        rhs, kd, qd = [], [], []
        for d in range(N_DIR):
            eg = jnp.exp(gcol[d])
            kb = k * beta[d]
            rhs.append(jnp.concatenate([v * beta[d], kb * eg], axis=1))
            kd.append((k * jnp.exp(glast[d] - gcol[d])).astype(BF16))
            qd.append(q * eg)
            eg_s[wslot, d, c * 8:(c + 1) * 8, :] = jnp.broadcast_to(jnp.exp(glast[d]), (8, LANES))
        return dict(l_p=l_p, rhs=jnp.concatenate(rhs, axis=0), intra=intra, kd=kd, qd=qd, c=c)

    def chunk_parallel(g):
        preps = [chunk_prep(g * GDN_UNROLL + j) for j in range(GDN_UNROLL)]
        yield
        l_ps = [p["l_p"] for p in preps]
        add = lambda xs, ys: [x + y for x, y in zip(xs, ys)]
        n = [-l_p * m16 for l_p in l_ps]
        xp = n
        for _ in range(3):
            xp = pmm(xp, xp)
            yield
            nx = pmm(n, xp)
            yield
            n = add(add(n, xp), nx)
        for m in (m32, m64):
            cm = [l_p * m for l_p in l_ps]
            y = add(cm, pmm(n, cm))
            yield
            yn = pmm(y, n)
            yield
            n = [a - b - c for a, b, c in zip(n, y, yn)]

        uw = [(p["rhs"] + _dot(blockdiag(a.astype(BF16)), p["rhs"].astype(BF16))).astype(BF16)
              for p, a in zip(preps, n)]
        yield
        iw = [_dot(blockdiag(p["intra"]), x) for p, x in zip(preps, uw)]
        yield
        kt = [[_dot_tn(p["kd"][d], x[d * C:(d + 1) * C]) for d in range(N_DIR)]
              for p, x in zip(preps, uw)]
        yield
        for p, w_iw, w_kt in zip(preps, iw, kt):
            c = p["c"]
            for d in range(N_DIR):
                b_s[wslot, d, c * DN_HD:(c + 1) * DN_HD, :] = w_kt[d][:, :LANES]
                aq_s[wslot, d, c * AQ_ROWS:c * AQ_ROWS + DN_HD, :] = (-w_kt[d][:, LANES:]).astype(BF16)
                aq_s[wslot, d, c * AQ_ROWS + DN_HD:(c + 1) * AQ_ROWS, :] = (
                    p["qd"][d] - w_iw[d * C:(d + 1) * C, LANES:]).astype(BF16)
            o0_s[wslot, c * C:(c + 1) * C, :] = w_iw[:C, :LANES] + w_iw[C:, :LANES]

    def recurrence_step(n, states):
        new_states = []
        for d in range(N_DIR):
            c = n if d == 0 else n_chunk - 1 - n
            st = states[d]
            r = _dot(aq_s[rslot, d, c * AQ_ROWS:(c + 1) * AQ_ROWS, :], st.astype(BF16))
            rows = slice(c * C, (c + 1) * C)
            seen = o0_s[rslot, rows, :] if n < n_chunk // 2 else o_ref[0, rows, :]
            o_ref[0, rows, :] = seen + r[DN_HD:]
            new_states.append(st * eg_s[rslot, d, c * 8:c * 8 + 1, :] + r[:DN_HD]
                              + b_s[rslot, d, c * DN_HD:(c + 1) * DN_HD, :])
        return new_states

    states = [jnp.zeros((DN_HD, DN_HD), F32)] * N_DIR
    for g in range(n_group):
        gen = chunk_parallel(g)
        done = 0
        for k in range(GDN_STAGES):
            next(gen, None)
            while done < (k + 1) * GDN_UNROLL // GDN_STAGES:
                states = recurrence_step(g * GDN_UNROLL + done, states)
                done += 1
        assert next(gen, "end") == "end" and done == GDN_UNROLL


def _gdn_call(dn, gates, batch, seq):
    dn3 = dn.reshape(batch, seq, 3 * DN_W)
    g3 = gates.reshape(batch, seq, GATE_PAD)
    blk = (1, seq, LANES)
    nh = N_DN_HEADS
    n_chunk = seq // CHUNK
    n_units = batch * nh
    assert seq % (CHUNK * GDN_UNROLL) == 0
    unit = lambda s: jnp.minimum(s, n_units - 1)
    prev = lambda s: jnp.maximum(s - 1, 0)
    return pl.pallas_call(
        functools.partial(_gdn_kernel, seq=seq, n_units=n_units),
        grid=(n_units + 1,),
        in_specs=[pl.BlockSpec(blk, lambda s: (unit(s) // nh, 0, unit(s) % nh)),
                  pl.BlockSpec(blk, lambda s: (unit(s) // nh, 0, nh + unit(s) % nh)),
                  pl.BlockSpec(blk, lambda s: (unit(s) // nh, 0, 2 * nh + unit(s) % nh)),
                  pl.BlockSpec(blk, lambda s: (unit(s) // nh, 0, 0))],
        out_specs=pl.BlockSpec(blk, lambda s: (prev(s) // nh, 0, prev(s) % nh)),
        out_shape=jax.ShapeDtypeStruct((batch, seq, DN_W), F32),
        scratch_shapes=[
            pltpu.VMEM((2, N_DIR, AQ_ROWS * n_chunk, LANES), BF16),
            pltpu.VMEM((2, N_DIR, DN_HD * n_chunk, LANES), F32),
            pltpu.VMEM((2, N_DIR, 8 * n_chunk, LANES), F32),
            pltpu.VMEM((2, seq, LANES), F32)],
        compiler_params=_cparams(1),
        name="gdn",
    )(dn3, dn3, dn3, g3)


def _mix_ffn_kernel(att_ref, attp_ref, attn_ref, dn_ref, dnp_ref, dnn_ref, z_ref, zp_ref, zn_ref,
                    x_ref, xp_ref, xn_ref, gn_ref, mod_ref, g_ref, wo_ref, wu_ref, cw_ref, cb_ref,
                    wd_ref, y_ref, u_s, act_s, *, per):
    i = pl.program_id(0)
    tm = x_ref.shape[0]
    rows = lambda p_ref, m_ref, n_ref: jnp.concatenate([p_ref[...], m_ref[...], n_ref[...]], axis=0)
    att, dn_raw, z, x = (rows(attp_ref, att_ref, attn_ref), rows(dnp_ref, dn_ref, dnn_ref),
                         rows(zp_ref, z_ref, zn_ref), rows(xp_ref, x_ref, xn_ref))
    heads = []
    for hd in range(N_DN_HEADS):
        cols = slice(hd * DN_HD, (hd + 1) * DN_HD)
        o = dn_raw[:, cols]
        o = o * lax.rsqrt(jnp.mean(o * o, axis=-1, keepdims=True) + NORM_EPS) * gn_ref[...]
        heads.append((o * _silu(z[:, cols].astype(F32))).astype(BF16))
    mix = _dot(att, wo_ref[0:ATT_W, :]) + _dot(jnp.concatenate(heads, axis=1), wo_ref[ATT_W:, :])
    x1 = x + mod_ref[0, 2:3, :] * mix
    ms = jnp.mean(x1 * x1, axis=-1, keepdims=True)
    hx = x1 * lax.rsqrt(ms + NORM_EPS) * g_ref[...]
    hx = (hx * (1.0 + mod_ref[0, 4:5, :]) + mod_ref[0, 3:4, :]).astype(BF16)
    row = lax.broadcasted_iota(jnp.int32, (tm + 2 * HALO, 1), 0)
    lo = jnp.where(i % per == 0, HALO, 0)
    hi = jnp.where(i % per == per - 1, HALO + tm, 2 * HALO + tm)
    hx = jnp.where((row < lo) | (row >= hi), jnp.zeros_like(hx), hx)
    x1 = x1[HALO:HALO + tm]
    n_chunk = D_FF // FFN_CHUNK

    def up(j):
        for half, o in enumerate((j * FFN_CHUNK, D_FF + j * FFN_CHUNK)):
            u_s[j % 2, half] = _dot(hx, wu_ref[:, o:o + FFN_CHUNK])

    def gate(j):
        halves = []
        for half, o in enumerate((j * FFN_CHUNK, D_FF + j * FFN_CHUNK)):
            cw = cw_ref[:, o:o + FFN_CHUNK]
            taps = [u_s[j % 2, half, HALO - 1 + k:HALO - 1 + k + tm, :] for k in range(3)]
            halves.append(taps[0] * cw[0:1, :] + taps[1] * cw[1:2, :] + taps[2] * cw[2:3, :]
                          + cb_ref[:, o:o + FFN_CHUNK])
        act_s[:, j * FFN_CHUNK:(j + 1) * FFN_CHUNK] = (_silu(halves[0]) * halves[1]).astype(BF16)

    acc = None
    start = 0
    up(0)
    for j in range(n_chunk):
        if j + 1 < n_chunk:
            up(j + 1)
        gate(j)
        if (j + 1) % FFN_DOWN_GROUP == 0 or j + 1 == n_chunk:
            lo, hi = start * FFN_CHUNK, (j + 1) * FFN_CHUNK
            part = _dot(act_s[:, lo:hi], wd_ref[lo:hi, :])
            acc = part if acc is None else acc + part
            start = j + 1
    y_ref[...] = x1 + mod_ref[0, 5:6, :] * acc


def _mix_ffn_call(att, dno, z, gn, x2, mod3, g2, w_o, w_up, conv_w, conv_b, w_down, seq):
    n_tok = x2.shape[0]
    tm = TOKEN_TILE
    per = seq // tm
    hb = tm // HALO
    last = n_tok // HALO - 1
    tok = lambda i: (i, 0)
    prev = lambda i: (jnp.maximum(i * hb - 1, 0), 0)
    nxt = lambda i: (jnp.minimum((i + 1) * hb, last), 0)
    halo3 = lambda width: [pl.BlockSpec((tm, width), tok), pl.BlockSpec((HALO, width), prev),
                           pl.BlockSpec((HALO, width), nxt)]
    return pl.pallas_call(
        functools.partial(_mix_ffn_kernel, per=per),
        grid=(n_tok // tm,),
        in_specs=halo3(ATT_W) + halo3(DN_W) + halo3(DN_W) + halo3(D_MODEL) + [
            _const_spec((1, DN_HD)),
            pl.BlockSpec((1, 6, D_MODEL), lambda i: (i // per, 0, 0)),
            _const_spec((1, D_MODEL)),
            _const_spec((D_MODEL, D_MODEL)),
            _const_spec((D_MODEL, 2 * D_FF)),
            _const_spec((3, 2 * D_FF)),
            _const_spec((1, 2 * D_FF)),
            _const_spec((D_FF, D_MODEL))],
        out_specs=pl.BlockSpec((tm, D_MODEL), tok),
        out_shape=jax.ShapeDtypeStruct((n_tok, D_MODEL), F32),
        scratch_shapes=[pltpu.VMEM((2, 2, tm + 2 * HALO, FFN_CHUNK), F32),
                        pltpu.VMEM((tm, D_FF), BF16)],
        compiler_params=_cparams(1),
        name="mix_ffn",
    )(att, att, att, dno, dno, dno, z, z, z, x2, x2, x2, gn, mod3, g2, w_o, w_up, conv_w, conv_b, w_down)


def _trunk(x, mod, p):
    batch, seq, d = x.shape
    assert seq % TOKEN_TILE == 0 and seq % GRID_W == 0
    x2 = x.reshape(batch * seq, d)
    mod3 = mod.reshape(batch, 6, d)
    att, dn, z, gates = _inproj_call(x2, mod3, p["norm1_g"], p["w_in"], p["dn_conv_w"], p["qg"], p["kg"],
                                     p["ones_att"], p["ones_dn"], p["alog_l"], p["dtb_l"], seq)
    att_o = _natten_call(att, p["bias_tbl"], batch, seq)
    dn_o = _gdn_call(dn, gates, batch, seq)
    y = _mix_ffn_call(att_o.reshape(batch * seq, ATT_W), dn_o.reshape(batch * seq, DN_W), z,
                      p["dn_out_norm"], x2, mod3, p["norm2_g"], p["w_o"], p["ffn_w_up"],
                      p["ffn_conv_w"], p["ffn_conv_b"], p["ffn_w_down"], seq)
    return y.reshape(batch, seq, d)


def kernel(x_prompt, x_sample, c_prompt, c_sample, ada_w, ada_b, norm1_g, norm2_g, w_in,
           att_q_norm, att_k_norm, att_rpb, dn_conv_w, dn_a_log, dn_dt_bias, dn_out_norm, w_o,
           ffn_w_up, ffn_conv_w, ffn_conv_b, ffn_w_down):
    depth = ada_w.shape[0]
    bp, bs = c_prompt.shape[0], c_sample.shape[0]
    n_c = -(-(bp + bs) // 8) * 8
    gate_lanes = lambda a: jnp.pad(a.reshape(1, N_DIR * N_DN_HEADS),
                                   ((0, 0), (N_DIR * N_DN_HEADS, GATE_PAD - N_GATE)))
    blk = np.arange(LANES) // ATT_HD
    ones_bd = jnp.asarray(blk[:, None] == blk[None, :], BF16)
    xs = [x_prompt, x_sample]
    for l in range(depth):
        p = {
            "norm1_g": norm1_g[l].reshape(1, -1), "norm2_g": norm2_g[l].reshape(1, -1),
            "w_in": jnp.pad(w_in[l], ((0, 0), (0, GATE_PAD - N_GATE))).astype(BF16),
            "alog_l": gate_lanes(dn_a_log[l]), "dtb_l": gate_lanes(dn_dt_bias[l]),
            "bias_tbl": _natten_bias_table(att_rpb[l]),
            "qg": jnp.tile(att_q_norm[l], N_ATT_HEADS).reshape(1, ATT_W),
            "kg": jnp.tile(att_k_norm[l], N_ATT_HEADS).reshape(1, ATT_W),
            "ones_att": ones_bd, "ones_dn": jnp.ones((LANES, LANES), BF16),
            "dn_conv_w": dn_conv_w[l], "dn_out_norm": dn_out_norm[l].reshape(1, -1),
            "w_o": w_o[l].astype(BF16), "ffn_w_up": ffn_w_up[l].astype(BF16),
            "ffn_conv_w": ffn_conv_w[l], "ffn_conv_b": ffn_conv_b[l].reshape(1, -1),
            "ffn_w_down": ffn_w_down[l].astype(BF16),
        }
        c_all = jnp.pad(jnp.concatenate([c_prompt, c_sample], axis=0), ((0, n_c - bp - bs), (0, 0)))
        mod = _mod_call(c_all, ada_w[l], ada_b[l])
        xs = [_trunk(xs[0], mod[:bp], p), _trunk(xs[1], mod[bp:bp + bs], p)]
    return tuple(xs)
```

```python
import functools

import numpy as np
import jax
import jax.numpy as jnp
from jax import lax
from jax.experimental import pallas as pl
from jax.experimental.pallas import tpu as pltpu

F32 = jnp.float32
BF16 = jnp.bfloat16

D_MODEL = 1024
GRID_W = 64
N_ATT_HEADS = 8
ATT_HD = 64
ATT_W = N_ATT_HEADS * ATT_HD
WIN_R = 8
WIN_C = 16
N_DN_HEADS = 4
DN_HD = 128
DN_W = N_DN_HEADS * DN_HD
N_DIR = 2
CHUNK = 64
D_FF = 2816
NORM_EPS = 1e-6
N_GATE = 2 * N_DIR * N_DN_HEADS
LANES = 128
GATE_PAD = LANES
IN_COLS_PAD = 3 * ATT_W + 4 * DN_W + GATE_PAD
MASK_NEG = -1e30

VMEM_LIMIT = 56 * 1024 * 1024
TOKEN_TILE = 512
FFN_CHUNK = 256
FFN_DOWN_GROUP = 4
HALO = 16
IN_CHUNK = 512
assert ATT_W == IN_CHUNK and DN_W == IN_CHUNK
SUBLANES = 8
AQ_ROWS = DN_HD + CHUNK
NATTEN_UNROLL = 16
GDN_UNROLL = 16
GDN_STAGES = 15


def _cparams(n_axes):
    return pltpu.CompilerParams(dimension_semantics=("arbitrary",) * n_axes,
                                vmem_limit_bytes=VMEM_LIMIT)


def _const_spec(shape):
    nd = len(shape)
    return pl.BlockSpec(shape, lambda *_: (0,) * nd, pipeline_mode=pl.Buffered(1))


def _silu(x):
    return x * jax.nn.sigmoid(x)


def _dot(a, b):
    return jnp.dot(a, b, preferred_element_type=F32)


def _dot_nt(a, b):
    return lax.dot_general(a, b, (((1,), (1,)), ((), ())), preferred_element_type=F32)


def _dot_tn(a, b):
    return lax.dot_general(a, b, (((0,), (0,)), ((), ())), preferred_element_type=F32)


def _mod_kernel(c_ref, w_ref, b_ref, o_ref):
    s = _silu(c_ref[...])
    o_ref[...] = jnp.dot(s, w_ref[...], preferred_element_type=F32,
                         precision=lax.Precision.HIGHEST) + b_ref[...]


def _mod_call(c, ada_w, ada_b):
    bp, d = c.shape
    n = ada_w.shape[1]
    bn = 1024
    return pl.pallas_call(
        _mod_kernel,
        grid=(n // bn,),
        in_specs=[pl.BlockSpec((bp, d), lambda j: (0, 0)),
                  pl.BlockSpec((d, bn), lambda j: (0, j)),
                  pl.BlockSpec((1, bn), lambda j: (0, j))],
        out_specs=pl.BlockSpec((bp, bn), lambda j: (0, j)),
        out_shape=jax.ShapeDtypeStruct((bp, n), F32),
        compiler_params=_cparams(1),
        name="mod",
    )(c, ada_w, ada_b.reshape(1, n))


def _inproj_kernel(x_ref, xp_ref, xn_ref, mod_ref, g_ref, w_ref, cw_ref, qg_ref, kg_ref,
                   ones_att_ref, ones_dn_ref, alog_ref, dtb_ref,
                   att_ref, dn_ref, z_ref, gate_ref, *, per):
    i = pl.program_id(0)
    tm = x_ref.shape[0]
    cw = IN_CHUNK

    def norm_mod(x):
        ms = jnp.mean(x * x, axis=-1, keepdims=True)
        y = x * lax.rsqrt(ms + NORM_EPS) * g_ref[...]
        return (y * (1.0 + mod_ref[0, 1:2, :]) + mod_ref[0, 0:1, :]).astype(BF16)

    zero = jnp.zeros((HALO, D_MODEL), BF16)
    hp = jnp.where(i % per != 0, norm_mod(xp_ref[...]), zero)
    hn = jnp.where(i % per != per - 1, norm_mod(xn_ref[...]), zero)
    hx = jnp.concatenate([hp, norm_mod(x_ref[...]), hn], axis=0)
    h = hx[HALO:HALO + tm]

    def group_scale(y, ones_ref, mean_div):
        outs = []
        for b in range(cw // LANES):
            yb = y[:, b * LANES:(b + 1) * LANES]
            ss = _dot((yb * yb).astype(BF16), ones_ref[...])
            outs.append(lax.rsqrt(ss * (1.0 / mean_div) + NORM_EPS))
        return jnp.concatenate(outs, axis=1)

    def att_qk(gain_ref, scale):
        def fin(u, o):
            r = u * group_scale(u, ones_att_ref, ATT_HD) * (gain_ref[...] * scale)
            att_ref[:, o:o + cw] = r.astype(BF16)
        return fin

    def att_v(u, o):
        att_ref[:, o:o + cw] = u.astype(BF16)

    def dn_conv(kind):
        def fin(u, o):
            wc = cw_ref[:, o:o + cw]
            prev = pltpu.roll(u, 1, 0)[HALO:HALO + tm]
            nxt = pltpu.roll(u, tm + 2 * HALO - 1, 0)[HALO:HALO + tm]
            y = _silu(prev * wc[0:1, :] + u[HALO:HALO + tm] * wc[1:2, :] + nxt * wc[2:3, :])
            if kind == "q":
                y = y * group_scale(y, ones_dn_ref, 1.0) * (DN_HD ** -0.5)
            elif kind == "k":
                y = y * group_scale(y, ones_dn_ref, 1.0)
            dn_ref[:, o:o + cw] = y.astype(BF16)
        return fin

    def z_out(u, o):
        z_ref[...] = u.astype(BF16)

    def gates(u, o):
        lane = lax.broadcasted_iota(jnp.int32, (1, GATE_PAD), 1)
        t = u + dtb_ref[...]
        softplus = jnp.maximum(t, 0.0) + jnp.log1p(jnp.exp(-jnp.abs(t)))
        decay = -jnp.exp(alog_ref[...]) * softplus
        gate_ref[...] = jnp.where(lane < N_DIR * N_DN_HEADS, jax.nn.sigmoid(u), decay)

    o_dn, o_z, o_g = 3 * ATT_W, 3 * ATT_W + 3 * DN_W, 3 * ATT_W + 4 * DN_W
    jobs = [(0, cw, h, 0, att_qk(qg_ref, ATT_HD ** -0.5)),
            (cw, cw, h, cw, att_qk(kg_ref, 1.0)),
            (2 * cw, cw, h, 2 * cw, att_v),
            (o_dn, cw, hx, 0, dn_conv("q")),
            (o_dn + cw, cw, hx, cw, dn_conv("k")),
            (o_dn + 2 * cw, cw, hx, 2 * cw, dn_conv("v")),
            (o_z, DN_W, h, 0, z_out),
            (o_g, GATE_PAD, h, 0, gates)]
    proj = lambda job: _dot(job[2], w_ref[:, job[0]:job[0] + job[1]])
    u_next = proj(jobs[0])
    for n, job in enumerate(jobs):
        u = u_next
        if n + 1 < len(jobs):
            u_next = proj(jobs[n + 1])
        job[4](u, job[3])


def _inproj_call(x2, mod3, g1, w_cat, conv_w, qg, kg, ones_att, ones_dn, alog_l, dtb_l, seq):
    n_tok = x2.shape[0]
    tm = TOKEN_TILE
    per = seq // tm
    hb = tm // HALO
    last = n_tok // HALO - 1
    tok = lambda i: (i, 0)
    return pl.pallas_call(
        functools.partial(_inproj_kernel, per=per),
        grid=(n_tok // tm,),
        in_specs=[pl.BlockSpec((tm, D_MODEL), tok),
                  pl.BlockSpec((HALO, D_MODEL), lambda i: (jnp.maximum(i * hb - 1, 0), 0)),
                  pl.BlockSpec((HALO, D_MODEL), lambda i: (jnp.minimum((i + 1) * hb, last), 0)),
                  pl.BlockSpec((1, 6, D_MODEL), lambda i: (i // per, 0, 0)),
                  _const_spec((1, D_MODEL)),
                  _const_spec((D_MODEL, IN_COLS_PAD)),
                  _const_spec((3, 3 * DN_W)),
                  _const_spec((1, IN_CHUNK)),
                  _const_spec((1, IN_CHUNK)),
                  _const_spec((LANES, LANES)),
                  _const_spec((LANES, LANES)),
                  _const_spec((1, GATE_PAD)),
                  _const_spec((1, GATE_PAD))],
        out_specs=[pl.BlockSpec((tm, 3 * ATT_W), tok),
                   pl.BlockSpec((tm, 3 * DN_W), tok),
                   pl.BlockSpec((tm, DN_W), tok),
                   pl.BlockSpec((tm, GATE_PAD), tok)],
        out_shape=[jax.ShapeDtypeStruct((n_tok, 3 * ATT_W), BF16),
                   jax.ShapeDtypeStruct((n_tok, 3 * DN_W), BF16),
                   jax.ShapeDtypeStruct((n_tok, DN_W), BF16),
                   jax.ShapeDtypeStruct((n_tok, GATE_PAD), F32)],
        compiler_params=_cparams(1),
        name="inproj",
    )(x2, x2, x2, mod3, g1, w_cat, conv_w, qg, kg, ones_att, ones_dn, alog_l, dtb_l)


def _natten_kernel(q_ref, k_ref, v_ref, bias_ref, o_ref, *, rows):
    lane = lax.broadcasted_iota(jnp.int32, (1, LANES), 1)
    first = lane < ATT_HD
    band = WIN_R * GRID_W

    def row_group(i, carry):
        units = []
        for j in range(NATTEN_UNROLL):
            r = i * NATTEN_UNROLL + j
            r0 = jnp.clip(r - WIN_R // 2, 0, rows - WIN_R)
            qs = pl.ds(pl.multiple_of(r * GRID_W, GRID_W), GRID_W)
            ks = pl.ds(pl.multiple_of(r0 * GRID_W, GRID_W), band)
            units.append((qs, ks, r - r0))
        scores = []
        for qs, ks, dd in units:
            q = q_ref[0, qs, :]
            q2 = jnp.concatenate([jnp.where(first, q, 0), jnp.where(first, 0, q)], axis=0)
            bias = bias_ref[:, dd].reshape(2 * GRID_W, band)
            scores.append(_dot_nt(q2, k_ref[0, ks, :]) + bias)
        probs = []
        for s in scores:
            p = jnp.exp(s - jnp.max(s, axis=-1, keepdims=True))
            probs.append((p.astype(BF16), jnp.sum(p, axis=-1, keepdims=True)))
        for (qs, ks, _), (p, l) in zip(units, probs):
            o2 = _dot(p, v_ref[0, ks, :]) / l
            o_ref[0, qs, :] = jnp.where(first, o2[:GRID_W], o2[GRID_W:]).astype(BF16)
        return carry

    lax.fori_loop(0, rows // NATTEN_UNROLL, row_group, 0)


def _natten_call(att, bias_tbl, batch, seq):
    rows = seq // GRID_W
    assert rows >= WIN_R and rows % NATTEN_UNROLL == 0
    att3 = att.reshape(batch, seq, 3 * ATT_W)
    n_pair = N_ATT_HEADS // 2
    blk = (1, seq, LANES)
    return pl.pallas_call(
        functools.partial(_natten_kernel, rows=rows),
        grid=(n_pair, batch),
        in_specs=[pl.BlockSpec(blk, lambda p, b: (b, 0, p)),
                  pl.BlockSpec(blk, lambda p, b: (b, 0, n_pair + p)),
                  pl.BlockSpec(blk, lambda p, b: (b, 0, 2 * n_pair + p)),
                  pl.BlockSpec((2, WIN_R, GRID_W, WIN_R * GRID_W), lambda p, b: (p, 0, 0, 0))],
        out_specs=pl.BlockSpec(blk, lambda p, b: (b, 0, p)),
        out_shape=jax.ShapeDtypeStruct((batch, seq, ATT_W), BF16),
        compiler_params=_cparams(2),
        name="natten",
    )(att3, att3, att3, bias_tbl)


def _natten_bias_table(rpb):
    n_row = 2 * WIN_R - 1
    per = 2 * GRID_W
    v = jnp.zeros((N_ATT_HEADS, n_row, per), F32)
    v = v.at[:, :, :WIN_C].set(rpb[:, :, WIN_C - 1:])
    v = v.at[:, :, per - (WIN_C - 1):].set(rpb[:, :, :WIN_C - 1])
    toep = jnp.tile(v, (1, 1, GRID_W))[:, :, :GRID_W * (per - 1)]
    toep = toep.reshape(N_ATT_HEADS, n_row, GRID_W, per - 1)[:, :, :, :GRID_W]
    qc = np.arange(GRID_W)[:, None]
    kc = np.arange(GRID_W)[None, :]
    c0 = np.clip(qc - WIN_C // 2, 0, GRID_W - WIN_C)
    valid = (kc >= c0) & (kc < c0 + WIN_C)
    toep = jnp.where(valid[None, None], toep, MASK_NEG)
    bands = [jnp.swapaxes(toep[:, WIN_R - 1 - dd:2 * WIN_R - 1 - dd], 1, 2) for dd in range(WIN_R)]
    tbl = jnp.stack(bands, axis=1)
    return tbl.reshape(N_ATT_HEADS, WIN_R, GRID_W, WIN_R * GRID_W)


def _gdn_kernel(q_ref, k_ref, v_ref, gate_ref, o_ref, aq_s, b_s, eg_s, o0_s, *, seq, n_units):
    s = pl.program_id(0)
    n_chunk = seq // CHUNK
    n_group = n_chunk // GDN_UNROLL
    head = jnp.minimum(s, n_units - 1) % N_DN_HEADS
    wslot = s % 2
    rslot = 1 - wslot
    C = CHUNK
    lane = lax.broadcasted_iota(jnp.int32, (1, LANES), 1)
    left = lane < C
    ri = lax.broadcasted_iota(jnp.int32, (C, LANES), 0)
    li = lax.broadcasted_iota(jnp.int32, (C, LANES), 1)
    ci = li & (C - 1)
    fwd_half = li < C
    lag = jnp.where(fwd_half, ri - ci, ci - ri)
    causal_p = lag >= 0
    strict_p = lag > 0
    eye_left = (li == ri)
    b16 = ((ci >> 4) == (ri >> 4)).astype(F32)
    b32 = ((ci >> 5) == (ri >> 5)).astype(F32)
    m16 = b16
    m32 = b32 - b16
    m64 = 1.0 - b32

    @pl.when(s == 0)
    def _():
        aq_s[1] = jnp.zeros(aq_s.shape[1:], aq_s.dtype)
        b_s[1] = jnp.zeros(b_s.shape[1:], b_s.dtype)
        eg_s[1] = jnp.zeros(eg_s.shape[1:], eg_s.dtype)
        o0_s[1] = jnp.zeros(o0_s.shape[1:], o0_s.dtype)

    def blockdiag(p16):
        return jnp.concatenate([jnp.where(left, p16, 0), jnp.where(left, 0, p16)], axis=0)

    def pmm(a_list, b_list):
        return [_dot(a.astype(BF16), blockdiag(b.astype(BF16))) for a, b in zip(a_list, b_list)]

    def lane_pick(x, col):
        return jnp.sum(jnp.where(lane == col, x, 0.0), axis=-1, keepdims=True)

    def chunk_prep(c):
        rows = slice(c * C, (c + 1) * C)
        q16 = q_ref[0, rows, :]
        k16 = k_ref[0, rows, :]
        q, k, v = q16.astype(F32), k16.astype(F32), v_ref[0, rows, :].astype(F32)
        gram = _dot_nt(jnp.concatenate([k16, q16], axis=0), k16)
        gkk = jnp.concatenate([gram[:C], gram[:C]], axis=1)
        gqk = jnp.concatenate([gram[C:], gram[C:]], axis=1)

        gat = gate_ref[0, rows, :]
        beta, gcol, grow, glast = [], [], [], []
        for d in range(N_DIR):
            beta.append(lane_pick(gat, d * N_DN_HEADS + head))
            g = lane_pick(gat, N_DIR * N_DN_HEADS + d * N_DN_HEADS + head)
            gb = jnp.broadcast_to(g, (C, LANES))
            inc = (ri <= ci) if d == 0 else (ri >= ci)
            gr = jnp.sum(jnp.where(inc, gb, 0.0), axis=0, keepdims=True)
            grow.append(gr)
            gcol.append(jnp.sum(jnp.where(eye_left, jnp.broadcast_to(gr, (C, LANES)), 0.0),
                                axis=-1, keepdims=True))
            glast.append(jnp.sum(g, axis=0, keepdims=True))
        beta_p = jnp.where(fwd_half, beta[0], beta[1])
        diff = jnp.where(fwd_half, gcol[0] - grow[0], gcol[1] - grow[1])
        decay = jnp.exp(jnp.where(causal_p, diff, MASK_NEG))
        l_p = jnp.where(strict_p, beta_p * gkk * decay, 0.0)
        intra = jnp.where(causal_p, gqk * decay, 0.0).astype(BF16)

        rhs, kd, qd = [], [], []
        for d in range(N_DIR):
            eg = jnp.exp(gcol[d])
            kb = k * beta[d]
            rhs.append(jnp.concatenate([v * beta[d], kb * eg], axis=1))
            kd.append((k * jnp.exp(glast[d] - gcol[d])).astype(BF16))
            qd.append((q * eg).astype(BF16))
            eg_s[wslot, d, c * SUBLANES:(c + 1) * SUBLANES, :] = jnp.broadcast_to(
                jnp.exp(glast[d]), (SUBLANES, LANES))
        return dict(l_p=l_p, rhs=jnp.concatenate(rhs, axis=0).astype(BF16), intra=intra, kd=kd, qd=qd, c=c)

    def chunk_parallel(g):
        preps = [chunk_prep(g * GDN_UNROLL + j) for j in range(GDN_UNROLL)]
        yield
        l_ps = [p["l_p"] for p in preps]
        add = lambda xs, ys: [x + y for x, y in zip(xs, ys)]
        n = [-l_p * m16 for l_p in l_ps]
        xp = n
        for _ in range(3):
            xp = pmm(xp, xp)
            yield
            nx = pmm(n, xp)
            yield
            n = add(add(n, xp), nx)
        for m in (m32, m64):
            cm = [l_p * m for l_p in l_ps]
            y = add(cm, pmm(n, cm))
            yield
            yn = pmm(y, n)
            yield
            n = [a - b - c for a, b, c in zip(n, y, yn)]

        uw = [(p["rhs"].astype(F32) + _dot(blockdiag(a.astype(BF16)), p["rhs"])).astype(BF16)
              for p, a in zip(preps, n)]
        yield
        iw = [_dot(blockdiag(p["intra"]), x) for p, x in zip(preps, uw)]
        yield
        kt = [[_dot_tn(p["kd"][d], x[d * C:(d + 1) * C]) for d in range(N_DIR)]
              for p, x in zip(preps, uw)]
        yield
        for p, w_iw, w_kt in zip(preps, iw, kt):
            c = p["c"]
            for d in range(N_DIR):
                b_s[wslot, d, c * DN_HD:(c + 1) * DN_HD, :] = w_kt[d][:, :LANES]
                aq_s[wslot, d, c * AQ_ROWS:c * AQ_ROWS + DN_HD, :] = (-w_kt[d][:, LANES:]).astype(BF16)
                aq_s[wslot, d, c * AQ_ROWS + DN_HD:(c + 1) * AQ_ROWS, :] = (
                    p["qd"][d].astype(F32) - w_iw[d * C:(d + 1) * C, LANES:]).astype(BF16)
            o0_s[wslot, c * C:(c + 1) * C, :] = w_iw[:C, :LANES] + w_iw[C:, :LANES]

    def recurrence_step(n, states):
        new_states = []
        for d in range(N_DIR):
            c = n if d == 0 else n_chunk - 1 - n
            st = states[d]
            r = _dot(aq_s[rslot, d, c * AQ_ROWS:(c + 1) * AQ_ROWS, :], st.astype(BF16))
            rows = slice(c * C, (c + 1) * C)
            seen = o0_s[rslot, rows, :] if n < n_chunk // 2 else o_ref[0, rows, :]
            o_ref[0, rows, :] = seen + r[DN_HD:]
            new_states.append(st * eg_s[rslot, d, c * SUBLANES:c * SUBLANES + 1, :] + r[:DN_HD]
                              + b_s[rslot, d, c * DN_HD:(c + 1) * DN_HD, :])
        return new_states

    states = [jnp.zeros((DN_HD, DN_HD), F32)] * N_DIR
    for g in range(n_group):
        gen = chunk_parallel(g)
        done = 0
        for k in range(GDN_STAGES):
            next(gen, None)
            while done < (k + 1) * GDN_UNROLL // GDN_STAGES:
                states = recurrence_step(g * GDN_UNROLL + done, states)
                done += 1
        assert next(gen, "end") == "end" and done == GDN_UNROLL


def _gdn_call(dn, gates, batch, seq):
    dn3 = dn.reshape(batch, seq, 3 * DN_W)
    g3 = gates.reshape(batch, seq, GATE_PAD)
    blk = (1, seq, LANES)
    nh = N_DN_HEADS
    n_chunk = seq // CHUNK
    n_units = batch * nh
    assert seq % (CHUNK * GDN_UNROLL) == 0
    unit = lambda s: jnp.minimum(s, n_units - 1)
    prev = lambda s: jnp.maximum(s - 1, 0)
    return pl.pallas_call(
        functools.partial(_gdn_kernel, seq=seq, n_units=n_units),
        grid=(n_units + 1,),
        in_specs=[pl.BlockSpec(blk, lambda s: (unit(s) // nh, 0, unit(s) % nh)),
                  pl.BlockSpec(blk, lambda s: (unit(s) // nh, 0, nh + unit(s) % nh)),
                  pl.BlockSpec(blk, lambda s: (unit(s) // nh, 0, 2 * nh + unit(s) % nh)),
                  pl.BlockSpec(blk, lambda s: (unit(s) // nh, 0, 0))],
        out_specs=pl.BlockSpec(blk, lambda s: (prev(s) // nh, 0, prev(s) % nh)),
        out_shape=jax.ShapeDtypeStruct((batch, seq, DN_W), F32),
        scratch_shapes=[
            pltpu.VMEM((2, N_DIR, AQ_ROWS * n_chunk, LANES), BF16),
            pltpu.VMEM((2, N_DIR, DN_HD * n_chunk, LANES), F32),
            pltpu.VMEM((2, N_DIR, SUBLANES * n_chunk, LANES), F32),
            pltpu.VMEM((2, seq, LANES), F32)],
        compiler_params=_cparams(1),
        name="gdn",
    )(dn3, dn3, dn3, g3)


def _mix_ffn_kernel(att_ref, attp_ref, attn_ref, dn_ref, dnp_ref, dnn_ref, z_ref, zp_ref, zn_ref,
                    x_ref, xp_ref, xn_ref, gn_ref, mod_ref, g_ref, wo_ref, wu_ref, cw_ref, cb_ref,
                    wd_ref, y_ref, u_s, act_s, *, per):
    i = pl.program_id(0)
    tm = x_ref.shape[0]
    rows = lambda p_ref, m_ref, n_ref: jnp.concatenate([p_ref[...], m_ref[...], n_ref[...]], axis=0)
    att, dn_raw, z, x = (rows(attp_ref, att_ref, attn_ref), rows(dnp_ref, dn_ref, dnn_ref),
                         rows(zp_ref, z_ref, zn_ref), rows(xp_ref, x_ref, xn_ref))
    heads = []
    for hd in range(N_DN_HEADS):
        cols = slice(hd * DN_HD, (hd + 1) * DN_HD)
        o = dn_raw[:, cols]
        o = o * lax.rsqrt(jnp.mean(o * o, axis=-1, keepdims=True) + NORM_EPS) * gn_ref[...]
        heads.append((o * _silu(z[:, cols].astype(F32))).astype(BF16))
    mix = _dot(att, wo_ref[0:ATT_W, :]) + _dot(jnp.concatenate(heads, axis=1), wo_ref[ATT_W:, :])
    x1 = x + mod_ref[0, 2:3, :] * mix
    ms = jnp.mean(x1 * x1, axis=-1, keepdims=True)
    hx = x1 * lax.rsqrt(ms + NORM_EPS) * g_ref[...]
    hx = (hx * (1.0 + mod_ref[0, 4:5, :]) + mod_ref[0, 3:4, :]).astype(BF16)
    row = lax.broadcasted_iota(jnp.int32, (tm + 2 * HALO, 1), 0)
    lo = jnp.where(i % per == 0, HALO, 0)
    hi = jnp.where(i % per == per - 1, HALO + tm, 2 * HALO + tm)
    hx = jnp.where((row < lo) | (row >= hi), jnp.zeros_like(hx), hx)
    x1 = x1[HALO:HALO + tm]
    n_chunk = D_FF // FFN_CHUNK

    def up(j):
        for half, o in enumerate((j * FFN_CHUNK, D_FF + j * FFN_CHUNK)):
            u_s[j % 2, half] = _dot(hx, wu_ref[:, o:o + FFN_CHUNK])

    def gate(j):
        halves = []
        for half, o in enumerate((j * FFN_CHUNK, D_FF + j * FFN_CHUNK)):
            cw = cw_ref[:, o:o + FFN_CHUNK]
            taps = [u_s[j % 2, half, HALO - 1 + k:HALO - 1 + k + tm, :] for k in range(3)]
            halves.append(taps[0] * cw[0:1, :] + taps[1] * cw[1:2, :] + taps[2] * cw[2:3, :]
                          + cb_ref[:, o:o + FFN_CHUNK])
        act_s[:, j * FFN_CHUNK:(j + 1) * FFN_CHUNK] = (_silu(halves[0]) * halves[1]).astype(BF16)

    acc = None
    start = 0
    up(0)
    for j in range(n_chunk):
        if j + 1 < n_chunk:
            up(j + 1)
        gate(j)
        if (j + 1) % FFN_DOWN_GROUP == 0 or j + 1 == n_chunk:
            lo, hi = start * FFN_CHUNK, (j + 1) * FFN_CHUNK
            part = _dot(act_s[:, lo:hi], wd_ref[lo:hi, :])
            acc = part if acc is None else acc + part
            start = j + 1
    y_ref[...] = x1 + mod_ref[0, 5:6, :] * acc


def _mix_ffn_call(att, dno, z, gn, x2, mod3, g2, w_o, w_up, conv_w, conv_b, w_down, seq):
    n_tok = x2.shape[0]
    tm = TOKEN_TILE
    per = seq // tm
    hb = tm // HALO
    last = n_tok // HALO - 1
    tok = lambda i: (i, 0)
    prev = lambda i: (jnp.maximum(i * hb - 1, 0), 0)
    nxt = lambda i: (jnp.minimum((i + 1) * hb, last), 0)
    halo3 = lambda width: [pl.BlockSpec((tm, width), tok), pl.BlockSpec((HALO, width), prev),
                           pl.BlockSpec((HALO, width), nxt)]
    return pl.pallas_call(
        functools.partial(_mix_ffn_kernel, per=per),
        grid=(n_tok // tm,),
        in_specs=halo3(ATT_W) + halo3(DN_W) + halo3(DN_W) + halo3(D_MODEL) + [
            _const_spec((1, DN_HD)),
            pl.BlockSpec((1, 6, D_MODEL), lambda i: (i // per, 0, 0)),
            _const_spec((1, D_MODEL)),
            _const_spec((D_MODEL, D_MODEL)),
            _const_spec((D_MODEL, 2 * D_FF)),
            _const_spec((3, 2 * D_FF)),
            _const_spec((1, 2 * D_FF)),
            _const_spec((D_FF, D_MODEL))],
        out_specs=pl.BlockSpec((tm, D_MODEL), tok),
        out_shape=jax.ShapeDtypeStruct((n_tok, D_MODEL), F32),
        scratch_shapes=[pltpu.VMEM((2, 2, tm + 2 * HALO, FFN_CHUNK), F32),
                        pltpu.VMEM((tm, D_FF), BF16)],
        compiler_params=_cparams(1),
        name="mix_ffn",
    )(att, att, att, dno, dno, dno, z, z, z, x2, x2, x2, gn, mod3, g2, w_o, w_up, conv_w, conv_b, w_down)


def _trunk(x, mod, p):
    batch, seq, d = x.shape
    assert seq % TOKEN_TILE == 0 and seq % GRID_W == 0
    x2 = x.reshape(batch * seq, d)
    mod3 = mod.reshape(batch, 6, d)
    att, dn, z, gates = _inproj_call(x2, mod3, p["norm1_g"], p["w_in"], p["dn_conv_w"], p["qg"], p["kg"],
                                     p["ones_att"], p["ones_dn"], p["alog_l"], p["dtb_l"], seq)
    att_o = _natten_call(att, p["bias_tbl"], batch, seq)
    dn_o = _gdn_call(dn, gates, batch, seq)
    y = _mix_ffn_call(att_o.reshape(batch * seq, ATT_W), dn_o.reshape(batch * seq, DN_W), z,
                      p["dn_out_norm"], x2, mod3, p["norm2_g"], p["w_o"], p["ffn_w_up"],
                      p["ffn_conv_w"], p["ffn_conv_b"], p["ffn_w_down"], seq)
    return y.reshape(batch, seq, d)


def kernel(x_prompt, x_sample, c_prompt, c_sample, ada_w, ada_b, norm1_g, norm2_g, w_in,
           att_q_norm, att_k_norm, att_rpb, dn_conv_w, dn_a_log, dn_dt_bias, dn_out_norm, w_o,
           ffn_w_up, ffn_conv_w, ffn_conv_b, ffn_w_down):
    depth = ada_w.shape[0]
    bp, bs = c_prompt.shape[0], c_sample.shape[0]
    n_c = -(-(bp + bs) // 8) * 8
    gate_lanes = lambda a: jnp.pad(a.reshape(1, N_DIR * N_DN_HEADS),
                                   ((0, 0), (N_DIR * N_DN_HEADS, GATE_PAD - N_GATE)))
    blk = np.arange(LANES) // ATT_HD
    ones_bd = jnp.asarray(blk[:, None] == blk[None, :], BF16)
    xs = [x_prompt, x_sample]
    for l in range(depth):
        p = {
            "norm1_g": norm1_g[l].reshape(1, -1), "norm2_g": norm2_g[l].reshape(1, -1),
            "w_in": jnp.pad(w_in[l], ((0, 0), (0, GATE_PAD - N_GATE))).astype(BF16),
            "alog_l": gate_lanes(dn_a_log[l]), "dtb_l": gate_lanes(dn_dt_bias[l]),
            "bias_tbl": _natten_bias_table(att_rpb[l]),
            "qg": jnp.tile(att_q_norm[l], N_ATT_HEADS).reshape(1, ATT_W),
            "kg": jnp.tile(att_k_norm[l], N_ATT_HEADS).reshape(1, ATT_W),
            "ones_att": ones_bd, "ones_dn": jnp.ones((LANES, LANES), BF16),
            "dn_conv_w": dn_conv_w[l], "dn_out_norm": dn_out_norm[l].reshape(1, -1),
            "w_o": w_o[l].astype(BF16), "ffn_w_up": ffn_w_up[l].astype(BF16),
            "ffn_conv_w": ffn_conv_w[l], "ffn_conv_b": ffn_conv_b[l].reshape(1, -1),
            "ffn_w_down": ffn_w_down[l].astype(BF16),
        }
        c_all = jnp.pad(jnp.concatenate([c_prompt, c_sample], axis=0), ((0, n_c - bp - bs), (0, 0)))
        mod = _mod_call(c_all, ada_w[l], ada_b[l])
        xs = [_trunk(xs[0], mod[:bp], p), _trunk(xs[1], mod[bp:bp + bs], p)]
    return tuple(xs)
```

```python
import functools

import numpy as np
import jax
import jax.numpy as jnp
from jax import lax
from jax.experimental import pallas as pl
from jax.experimental.pallas import tpu as pltpu

F32 = jnp.float32
BF16 = jnp.bfloat16

D_MODEL = 1024
GRID_W = 64
N_ATT_HEADS = 8
ATT_HD = 64
ATT_W = N_ATT_HEADS * ATT_HD
WIN_R = 8
WIN_C = 16
N_DN_HEADS = 4
DN_HD = 128
DN_W = N_DN_HEADS * DN_HD
N_DIR = 2
CHUNK = 64
D_FF = 2816
NORM_EPS = 1e-6
N_GATE = 2 * N_DIR * N_DN_HEADS
LANES = 128
GATE_PAD = LANES
IN_COLS_PAD = 3 * ATT_W + 4 * DN_W + GATE_PAD
MASK_NEG = -1e30

VMEM_LIMIT = 56 * 1024 * 1024
TOKEN_TILE = 512
MOD_BLOCK = 1024
FFN_CHUNK = 256
FFN_DOWN_SPLITS = (9, D_FF // FFN_CHUNK)
assert FFN_DOWN_SPLITS[-1] == D_FF // FFN_CHUNK
HALO = 16
IN_CHUNK = 512
assert ATT_W == IN_CHUNK and DN_W == IN_CHUNK
SUBLANES = 8
AQ_ROWS = DN_HD + CHUNK
NATTEN_UNROLL = 16
GDN_UNROLL = 16
GDN_STAGES = 15


def _cparams(n_axes):
    return pltpu.CompilerParams(dimension_semantics=("arbitrary",) * n_axes,
                                vmem_limit_bytes=VMEM_LIMIT)


def _const_spec(shape):
    nd = len(shape)
    return pl.BlockSpec(shape, lambda *_: (0,) * nd, pipeline_mode=pl.Buffered(1))


def _silu(x):
    return x * jax.nn.sigmoid(x)


def _dot(a, b):
    return jnp.dot(a, b, preferred_element_type=F32)


def _dot_nt(a, b):
    return lax.dot_general(a, b, (((1,), (1,)), ((), ())), preferred_element_type=F32)


def _dot_tn(a, b):
    return lax.dot_general(a, b, (((0,), (0,)), ((), ())), preferred_element_type=F32)


def _mod_kernel(c_ref, w_ref, b_ref, o_ref):
    s = _silu(c_ref[...])
    o_ref[...] = jnp.dot(s, w_ref[...], preferred_element_type=F32,
                         precision=lax.Precision.HIGHEST) + b_ref[...]


def _mod_call(c, ada_w, ada_b):
    bp, d = c.shape
    n = ada_w.shape[1]
    bn = MOD_BLOCK
    return pl.pallas_call(
        _mod_kernel,
        grid=(n // bn,),
        in_specs=[pl.BlockSpec((bp, d), lambda j: (0, 0)),
                  pl.BlockSpec((d, bn), lambda j: (0, j)),
                  pl.BlockSpec((1, bn), lambda j: (0, j))],
        out_specs=pl.BlockSpec((bp, bn), lambda j: (0, j)),
        out_shape=jax.ShapeDtypeStruct((bp, n), F32),
        compiler_params=_cparams(1),
        name="mod",
    )(c, ada_w, ada_b.reshape(1, n))


def _inproj_kernel(x_ref, xp_ref, xn_ref, mod_ref, g_ref, w_ref, cw_ref, qg_ref, kg_ref,
                   ones_att_ref, ones_dn_ref, alog_ref, dtb_ref,
                   att_ref, dn_ref, z_ref, gate_ref, *, per):
    i = pl.program_id(0)
    tm = x_ref.shape[0]
    cw = IN_CHUNK

    def norm_mod(x):
        ms = jnp.mean(x * x, axis=-1, keepdims=True)
        y = x * lax.rsqrt(ms + NORM_EPS) * g_ref[...]
        return (y * (1.0 + mod_ref[0, 1:2, :]) + mod_ref[0, 0:1, :]).astype(BF16)

    zero = jnp.zeros((HALO, D_MODEL), BF16)
    hp = jnp.where(i % per != 0, norm_mod(xp_ref[...]), zero)
    hn = jnp.where(i % per != per - 1, norm_mod(xn_ref[...]), zero)
    hx = jnp.concatenate([hp, norm_mod(x_ref[...]), hn], axis=0)
    h = hx[HALO:HALO + tm]

    def group_scale(y, ones_ref, mean_div):
        outs = []
        for b in range(cw // LANES):
            yb = y[:, b * LANES:(b + 1) * LANES]
            ss = _dot((yb * yb).astype(BF16), ones_ref[...])
            outs.append(lax.rsqrt(ss * (1.0 / mean_div) + NORM_EPS))
        return jnp.concatenate(outs, axis=1)

    def att_qk(gain_ref, scale):
        def fin(u, o):
            r = u * group_scale(u, ones_att_ref, ATT_HD) * (gain_ref[...] * scale)
            att_ref[:, o:o + cw] = r.astype(BF16)
        return fin

    def att_v(u, o):
        att_ref[:, o:o + cw] = u.astype(BF16)

    def dn_conv(kind):
        def fin(u, o):
            wc = cw_ref[:, o:o + cw]
            prev = pltpu.roll(u, 1, 0)[HALO:HALO + tm]
            nxt = pltpu.roll(u, tm + 2 * HALO - 1, 0)[HALO:HALO + tm]
            y = _silu(prev * wc[0:1, :] + u[HALO:HALO + tm] * wc[1:2, :] + nxt * wc[2:3, :])
            if kind == "q":
                y = y * group_scale(y, ones_dn_ref, 1.0) * (DN_HD ** -0.5)
            elif kind == "k":
                y = y * group_scale(y, ones_dn_ref, 1.0)
            dn_ref[:, o:o + cw] = y.astype(BF16)
        return fin

    def z_out(u, o):
        z_ref[...] = u.astype(BF16)

    def gates(u, o):
        lane = lax.broadcasted_iota(jnp.int32, (1, GATE_PAD), 1)
        t = u + dtb_ref[...]
        softplus = jnp.maximum(t, 0.0) + jnp.log1p(jnp.exp(-jnp.abs(t)))
        decay = -jnp.exp(alog_ref[...]) * softplus
        gate_ref[...] = jnp.where(lane < N_DIR * N_DN_HEADS, jax.nn.sigmoid(u), decay)

    o_dn, o_z, o_g = 3 * ATT_W, 3 * ATT_W + 3 * DN_W, 3 * ATT_W + 4 * DN_W
    jobs = [(0, cw, h, 0, att_qk(qg_ref, ATT_HD ** -0.5)),
            (cw, cw, h, cw, att_qk(kg_ref, 1.0)),
            (2 * cw, cw, h, 2 * cw, att_v),
            (o_dn, cw, hx, 0, dn_conv("q")),
            (o_dn + cw, cw, hx, cw, dn_conv("k")),
            (o_dn + 2 * cw, cw, hx, 2 * cw, dn_conv("v")),
            (o_z, DN_W, h, 0, z_out),
            (o_g, GATE_PAD, h, 0, gates)]
    proj = lambda job: _dot(job[2], w_ref[:, job[0]:job[0] + job[1]])
    u_next = proj(jobs[0])
    for n, job in enumerate(jobs):
        u = u_next
        if n + 1 < len(jobs):
            u_next = proj(jobs[n + 1])
        job[4](u, job[3])


def _inproj_call(x2, mod3, g1, w_cat, conv_w, qg, kg, ones_att, ones_dn, alog_l, dtb_l, seq):
    n_tok = x2.shape[0]
    tm = TOKEN_TILE
    per = seq // tm
    hb = tm // HALO
    last = n_tok // HALO - 1
    tok = lambda i: (i, 0)
    return pl.pallas_call(
        functools.partial(_inproj_kernel, per=per),
        grid=(n_tok // tm,),
        in_specs=[pl.BlockSpec((tm, D_MODEL), tok),
                  pl.BlockSpec((HALO, D_MODEL), lambda i: (jnp.maximum(i * hb - 1, 0), 0)),
                  pl.BlockSpec((HALO, D_MODEL), lambda i: (jnp.minimum((i + 1) * hb, last), 0)),
                  pl.BlockSpec((1, 6, D_MODEL), lambda i: (i // per, 0, 0)),
                  _const_spec((1, D_MODEL)),
                  _const_spec((D_MODEL, IN_COLS_PAD)),
                  _const_spec((3, 3 * DN_W)),
                  _const_spec((1, IN_CHUNK)),
                  _const_spec((1, IN_CHUNK)),
                  _const_spec((LANES, LANES)),
                  _const_spec((LANES, LANES)),
                  _const_spec((1, GATE_PAD)),
                  _const_spec((1, GATE_PAD))],
        out_specs=[pl.BlockSpec((tm, 3 * ATT_W), tok),
                   pl.BlockSpec((tm, 3 * DN_W), tok),
                   pl.BlockSpec((tm, DN_W), tok),
                   pl.BlockSpec((tm, GATE_PAD), tok)],
        out_shape=[jax.ShapeDtypeStruct((n_tok, 3 * ATT_W), BF16),
                   jax.ShapeDtypeStruct((n_tok, 3 * DN_W), BF16),
                   jax.ShapeDtypeStruct((n_tok, DN_W), BF16),
                   jax.ShapeDtypeStruct((n_tok, GATE_PAD), F32)],
        compiler_params=_cparams(1),
        name="inproj",
    )(x2, x2, x2, mod3, g1, w_cat, conv_w, qg, kg, ones_att, ones_dn, alog_l, dtb_l)


def _natten_kernel(q_ref, k_ref, v_ref, bias_ref, o_ref, *, rows):
    lane = lax.broadcasted_iota(jnp.int32, (1, LANES), 1)
    first = lane < ATT_HD
    band = WIN_R * GRID_W

    def row_group(i, carry):
        units = []
        for j in range(NATTEN_UNROLL):
            r = i * NATTEN_UNROLL + j
            r0 = jnp.clip(r - WIN_R // 2, 0, rows - WIN_R)
            qs = pl.ds(pl.multiple_of(r * GRID_W, GRID_W), GRID_W)
            ks = pl.ds(pl.multiple_of(r0 * GRID_W, GRID_W), band)
            units.append((qs, ks, r - r0))
        scores = []
        for qs, ks, dd in units:
            q = q_ref[0, qs, :]
            q2 = jnp.concatenate([jnp.where(first, q, 0), jnp.where(first, 0, q)], axis=0)
            bias = bias_ref[:, dd].reshape(2 * GRID_W, band)
            scores.append(_dot_nt(q2, k_ref[0, ks, :]) + bias)
        probs = []
        for s in scores:
            p = jnp.exp(s - jnp.max(s, axis=-1, keepdims=True))
            probs.append((p.astype(BF16), jnp.sum(p, axis=-1, keepdims=True)))
        for (qs, ks, _), (p, l) in zip(units, probs):
            o2 = _dot(p, v_ref[0, ks, :]) / l
            o_ref[0, qs, :] = jnp.where(first, o2[:GRID_W], o2[GRID_W:]).astype(BF16)
        return carry

    lax.fori_loop(0, rows // NATTEN_UNROLL, row_group, 0)


def _natten_call(att, bias_tbl, batch, seq):
    rows = seq // GRID_W
    assert rows >= WIN_R and rows % NATTEN_UNROLL == 0
    att3 = att.reshape(batch, seq, 3 * ATT_W)
    n_pair = N_ATT_HEADS // 2
    blk = (1, seq, LANES)
    return pl.pallas_call(
        functools.partial(_natten_kernel, rows=rows),
        grid=(n_pair, batch),
        in_specs=[pl.BlockSpec(blk, lambda p, b: (b, 0, p)),
                  pl.BlockSpec(blk, lambda p, b: (b, 0, n_pair + p)),
                  pl.BlockSpec(blk, lambda p, b: (b, 0, 2 * n_pair + p)),
                  pl.BlockSpec((2, WIN_R, GRID_W, WIN_R * GRID_W), lambda p, b: (p, 0, 0, 0))],
        out_specs=pl.BlockSpec(blk, lambda p, b: (b, 0, p)),
        out_shape=jax.ShapeDtypeStruct((batch, seq, ATT_W), BF16),
        compiler_params=_cparams(2),
        name="natten",
    )(att3, att3, att3, bias_tbl)


def _natten_bias_table(rpb):
    n_row = 2 * WIN_R - 1
    per = 2 * GRID_W
    v = jnp.zeros((N_ATT_HEADS, n_row, per), F32)
    v = v.at[:, :, :WIN_C].set(rpb[:, :, WIN_C - 1:])
    v = v.at[:, :, per - (WIN_C - 1):].set(rpb[:, :, :WIN_C - 1])
    toep = jnp.tile(v, (1, 1, GRID_W))[:, :, :GRID_W * (per - 1)]
    toep = toep.reshape(N_ATT_HEADS, n_row, GRID_W, per - 1)[:, :, :, :GRID_W]
    qc = np.arange(GRID_W)[:, None]
    kc = np.arange(GRID_W)[None, :]
    c0 = np.clip(qc - WIN_C // 2, 0, GRID_W - WIN_C)
    valid = (kc >= c0) & (kc < c0 + WIN_C)
    toep = jnp.where(valid[None, None], toep, MASK_NEG)
    bands = [jnp.swapaxes(toep[:, WIN_R - 1 - dd:2 * WIN_R - 1 - dd], 1, 2) for dd in range(WIN_R)]
    tbl = jnp.stack(bands, axis=1)
    return tbl.reshape(N_ATT_HEADS, WIN_R, GRID_W, WIN_R * GRID_W)


def _gdn_kernel(q_ref, k_ref, v_ref, gate_ref, o_ref, aq_s, b_s, eg_s, o0_s, *, seq, n_units):
    s = pl.program_id(0)
    n_chunk = seq // CHUNK
    n_group = n_chunk // GDN_UNROLL
    head = jnp.minimum(s, n_units - 1) % N_DN_HEADS
    wslot = s % 2
    rslot = 1 - wslot
    C = CHUNK
    lane = lax.broadcasted_iota(jnp.int32, (1, LANES), 1)
    left = lane < C
    ri = lax.broadcasted_iota(jnp.int32, (C, LANES), 0)
    li = lax.broadcasted_iota(jnp.int32, (C, LANES), 1)
    ci = li & (C - 1)
    fwd_half = li < C
    lag = jnp.where(fwd_half, ri - ci, ci - ri)
    causal_p = lag >= 0
    strict_p = lag > 0
    eye_left = (li == ri)
    b16 = ((ci >> 4) == (ri >> 4)).astype(F32)
    b32 = ((ci >> 5) == (ri >> 5)).astype(F32)
    m16 = b16
    m32 = b32 - b16
    m64 = 1.0 - b32

    @pl.when(s == 0)
    def _():
        aq_s[1] = jnp.zeros(aq_s.shape[1:], aq_s.dtype)
        b_s[1] = jnp.zeros(b_s.shape[1:], b_s.dtype)
        eg_s[1] = jnp.zeros(eg_s.shape[1:], eg_s.dtype)
        o0_s[1] = jnp.zeros(o0_s.shape[1:], o0_s.dtype)

    def blockdiag(p16):
        return jnp.concatenate([jnp.where(left, p16, 0), jnp.where(left, 0, p16)], axis=0)

    def pmm(a_list, b_list):
        return [_dot(a.astype(BF16), blockdiag(b.astype(BF16))) for a, b in zip(a_list, b_list)]

    def lane_pick(x, col):
        return jnp.sum(jnp.where(lane == col, x, 0.0), axis=-1, keepdims=True)

    def chunk_prep(c):
        rows = slice(c * C, (c + 1) * C)
        q16 = q_ref[0, rows, :]
        k16 = k_ref[0, rows, :]
        q, k, v = q16.astype(F32), k16.astype(F32), v_ref[0, rows, :].astype(F32)
        gram = _dot_nt(jnp.concatenate([k16, q16], axis=0), k16)
        gkk = jnp.concatenate([gram[:C], gram[:C]], axis=1)
        gqk = jnp.concatenate([gram[C:], gram[C:]], axis=1)

        gat = gate_ref[0, rows, :]
        beta, gcol, grow, glast = [], [], [], []
        for d in range(N_DIR):
            beta.append(lane_pick(gat, d * N_DN_HEADS + head))
            g = lane_pick(gat, N_DIR * N_DN_HEADS + d * N_DN_HEADS + head)
            gb = jnp.broadcast_to(g, (C, LANES))
            inc = (ri <= ci) if d == 0 else (ri >= ci)
            gr = jnp.sum(jnp.where(inc, gb, 0.0), axis=0, keepdims=True)
            grow.append(gr)
            gcol.append(jnp.sum(jnp.where(eye_left, jnp.broadcast_to(gr, (C, LANES)), 0.0),
                                axis=-1, keepdims=True))
            glast.append(jnp.sum(g, axis=0, keepdims=True))
        beta_p = jnp.where(fwd_half, beta[0], beta[1])
        diff = jnp.where(fwd_half, gcol[0] - grow[0], gcol[1] - grow[1])
        decay = jnp.exp(jnp.where(causal_p, diff, MASK_NEG))
        l_p = jnp.where(strict_p, beta_p * gkk * decay, 0.0)
        intra = jnp.where(causal_p, gqk * decay, 0.0).astype(BF16)

        rhs, kd, qd = [], [], []
        for d in range(N_DIR):
            eg = jnp.exp(gcol[d])
            kb = k * beta[d]
            rhs.append(jnp.concatenate([v * beta[d], kb * eg], axis=1))
            kd.append((k * jnp.exp(glast[d] - gcol[d])).astype(BF16))
            qd.append((q * eg).astype(BF16))
            eg_s[wslot, d, c * SUBLANES:(c + 1) * SUBLANES, :] = jnp.broadcast_to(
                jnp.exp(glast[d]), (SUBLANES, LANES))
        return dict(l_p=l_p, rhs=jnp.concatenate(rhs, axis=0).astype(BF16), intra=intra, kd=kd, qd=qd, c=c)

    def chunk_parallel(g):
        preps = [chunk_prep(g * GDN_UNROLL + j) for j in range(GDN_UNROLL)]
        yield
        l_ps = [p["l_p"] for p in preps]
        add = lambda xs, ys: [x + y for x, y in zip(xs, ys)]
        n = [-l_p * m16 for l_p in l_ps]
        xp = n
        for _ in range(3):
            xp = pmm(xp, xp)
            yield
            nx = pmm(n, xp)
            yield
            n = add(add(n, xp), nx)
        for m in (m32, m64):
            cm = [l_p * m for l_p in l_ps]
            y = add(cm, pmm(n, cm))
            yield
            yn = pmm(y, n)
            yield
            n = [a - b - c for a, b, c in zip(n, y, yn)]

        uw = [(p["rhs"].astype(F32) + _dot(blockdiag(a.astype(BF16)), p["rhs"])).astype(BF16)
              for p, a in zip(preps, n)]
        yield
        iw = [_dot(blockdiag(p["intra"]), x) for p, x in zip(preps, uw)]
        yield
        kt = [[_dot_tn(p["kd"][d], x[d * C:(d + 1) * C]) for d in range(N_DIR)]
              for p, x in zip(preps, uw)]
        yield
        for p, w_iw, w_kt in zip(preps, iw, kt):
            c = p["c"]
            for d in range(N_DIR):
                b_s[wslot, d, c * DN_HD:(c + 1) * DN_HD, :] = w_kt[d][:, :LANES]
                aq_s[wslot, d, c * AQ_ROWS:c * AQ_ROWS + DN_HD, :] = (-w_kt[d][:, LANES:]).astype(BF16)
                aq_s[wslot, d, c * AQ_ROWS + DN_HD:(c + 1) * AQ_ROWS, :] = (
                    p["qd"][d].astype(F32) - w_iw[d * C:(d + 1) * C, LANES:]).astype(BF16)
            o0_s[wslot, c * C:(c + 1) * C, :] = w_iw[:C, :LANES] + w_iw[C:, :LANES]

    def recurrence_step(n, states):
        new_states = []
        for d in range(N_DIR):
            c = n if d == 0 else n_chunk - 1 - n
            st = states[d]
            r = _dot(aq_s[rslot, d, c * AQ_ROWS:(c + 1) * AQ_ROWS, :], st.astype(BF16))
            rows = slice(c * C, (c + 1) * C)
            seen = o0_s[rslot, rows, :] if n < n_chunk // 2 else o_ref[0, rows, :]
            o_ref[0, rows, :] = seen + r[DN_HD:]
            new_states.append(st * eg_s[rslot, d, c * SUBLANES:c * SUBLANES + 1, :] + r[:DN_HD]
                              + b_s[rslot, d, c * DN_HD:(c + 1) * DN_HD, :])
        return new_states

    states = [jnp.zeros((DN_HD, DN_HD), F32)] * N_DIR
    for g in range(n_group):
        gen = chunk_parallel(g)
        done = 0
        for k in range(GDN_STAGES):
            next(gen, None)
            while done < (k + 1) * GDN_UNROLL // GDN_STAGES:
                states = recurrence_step(g * GDN_UNROLL + done, states)
                done += 1
        assert next(gen, "end") == "end" and done == GDN_UNROLL


def _gdn_call(dn, gates, batch, seq):
    dn3 = dn.reshape(batch, seq, 3 * DN_W)
    g3 = gates.reshape(batch, seq, GATE_PAD)
    blk = (1, seq, LANES)
    nh = N_DN_HEADS
    n_chunk = seq // CHUNK
    n_units = batch * nh
    assert seq % (CHUNK * GDN_UNROLL) == 0
    unit = lambda s: jnp.minimum(s, n_units - 1)
    prev = lambda s: jnp.maximum(s - 1, 0)
    return pl.pallas_call(
        functools.partial(_gdn_kernel, seq=seq, n_units=n_units),
        grid=(n_units + 1,),
        in_specs=[pl.BlockSpec(blk, lambda s: (unit(s) // nh, 0, unit(s) % nh)),
                  pl.BlockSpec(blk, lambda s: (unit(s) // nh, 0, nh + unit(s) % nh)),
                  pl.BlockSpec(blk, lambda s: (unit(s) // nh, 0, 2 * nh + unit(s) % nh)),
                  pl.BlockSpec(blk, lambda s: (unit(s) // nh, 0, 0))],
        out_specs=pl.BlockSpec(blk, lambda s: (prev(s) // nh, 0, prev(s) % nh)),
        out_shape=jax.ShapeDtypeStruct((batch, seq, DN_W), F32),
        scratch_shapes=[
            pltpu.VMEM((2, N_DIR, AQ_ROWS * n_chunk, LANES), BF16),
            pltpu.VMEM((2, N_DIR, DN_HD * n_chunk, LANES), F32),
            pltpu.VMEM((2, N_DIR, SUBLANES * n_chunk, LANES), F32),
            pltpu.VMEM((2, seq, LANES), F32)],
        compiler_params=_cparams(1),
        name="gdn",
    )(dn3, dn3, dn3, g3)


def _mix_ffn_kernel(att_ref, attp_ref, attn_ref, dn_ref, dnp_ref, dnn_ref, z_ref, zp_ref, zn_ref,
                    x_ref, xp_ref, xn_ref, gn_ref, mod_ref, g_ref, wo_ref, wu_ref, cw_ref, cb_ref,
                    wd_ref, y_ref, u_s, act_s, *, per):
    i = pl.program_id(0)
    tm = x_ref.shape[0]
    rows = lambda p_ref, m_ref, n_ref: jnp.concatenate([p_ref[...], m_ref[...], n_ref[...]], axis=0)
    att, dn_raw, z, x = (rows(attp_ref, att_ref, attn_ref), rows(dnp_ref, dn_ref, dnn_ref),
                         rows(zp_ref, z_ref, zn_ref), rows(xp_ref, x_ref, xn_ref))
    heads = []
    for hd in range(N_DN_HEADS):
        cols = slice(hd * DN_HD, (hd + 1) * DN_HD)
        o = dn_raw[:, cols]
        o = o * lax.rsqrt(jnp.mean(o * o, axis=-1, keepdims=True) + NORM_EPS) * gn_ref[...]
        heads.append((o * _silu(z[:, cols].astype(F32))).astype(BF16))
    mix = _dot(att, wo_ref[0:ATT_W, :]) + _dot(jnp.concatenate(heads, axis=1), wo_ref[ATT_W:, :])
    x1 = x + mod_ref[0, 2:3, :] * mix
    ms = jnp.mean(x1 * x1, axis=-1, keepdims=True)
    hx = x1 * lax.rsqrt(ms + NORM_EPS) * g_ref[...]
    hx = (hx * (1.0 + mod_ref[0, 4:5, :]) + mod_ref[0, 3:4, :]).astype(BF16)
    row = lax.broadcasted_iota(jnp.int32, (tm + 2 * HALO, 1), 0)
    lo = jnp.where(i % per == 0, HALO, 0)
    hi = jnp.where(i % per == per - 1, HALO + tm, 2 * HALO + tm)
    hx = jnp.where((row < lo) | (row >= hi), jnp.zeros_like(hx), hx)
    x1 = x1[HALO:HALO + tm]
    n_chunk = D_FF // FFN_CHUNK

    def up(j):
        for half, o in enumerate((j * FFN_CHUNK, D_FF + j * FFN_CHUNK)):
            u_s[j % 2, half] = _dot(hx, wu_ref[:, o:o + FFN_CHUNK])

    def gate(j):
        halves = []
        for half, o in enumerate((j * FFN_CHUNK, D_FF + j * FFN_CHUNK)):
            cw = cw_ref[:, o:o + FFN_CHUNK]
            taps = [u_s[j % 2, half, HALO - 1 + k:HALO - 1 + k + tm, :] for k in range(3)]
            halves.append(taps[0] * cw[0:1, :] + taps[1] * cw[1:2, :] + taps[2] * cw[2:3, :]
                          + cb_ref[:, o:o + FFN_CHUNK])
        act_s[:, j * FFN_CHUNK:(j + 1) * FFN_CHUNK] = (_silu(halves[0]) * halves[1]).astype(BF16)

    acc = None
    start = 0
    up(0)
    for j in range(n_chunk):
        if j + 1 < n_chunk:
            up(j + 1)
        gate(j)
        if j + 1 in FFN_DOWN_SPLITS:
            lo, hi = start * FFN_CHUNK, (j + 1) * FFN_CHUNK
            part = _dot(act_s[:, lo:hi], wd_ref[lo:hi, :])
            acc = part if acc is None else acc + part
            start = j + 1
    y_ref[...] = x1 + mod_ref[0, 5:6, :] * acc


def _mix_ffn_call(att, dno, z, gn, x2, mod3, g2, w_o, w_up, conv_w, conv_b, w_down, seq):
    n_tok = x2.shape[0]
    tm = TOKEN_TILE
    per = seq // tm
    hb = tm // HALO
    last = n_tok // HALO - 1
    tok = lambda i: (i, 0)
    prev = lambda i: (jnp.maximum(i * hb - 1, 0), 0)
    nxt = lambda i: (jnp.minimum((i + 1) * hb, last), 0)
    halo3 = lambda width: [pl.BlockSpec((tm, width), tok), pl.BlockSpec((HALO, width), prev),
                           pl.BlockSpec((HALO, width), nxt)]
    return pl.pallas_call(
        functools.partial(_mix_ffn_kernel, per=per),
        grid=(n_tok // tm,),
        in_specs=halo3(ATT_W) + halo3(DN_W) + halo3(DN_W) + halo3(D_MODEL) + [
            _const_spec((1, DN_HD)),
            pl.BlockSpec((1, 6, D_MODEL), lambda i: (i // per, 0, 0)),
            _const_spec((1, D_MODEL)),
            _const_spec((D_MODEL, D_MODEL)),
            _const_spec((D_MODEL, 2 * D_FF)),
            _const_spec((3, 2 * D_FF)),
            _const_spec((1, 2 * D_FF)),
            _const_spec((D_FF, D_MODEL))],
        out_specs=pl.BlockSpec((tm, D_MODEL), tok),
        out_shape=jax.ShapeDtypeStruct((n_tok, D_MODEL), F32),
        scratch_shapes=[pltpu.VMEM((2, 2, tm + 2 * HALO, FFN_CHUNK), F32),
                        pltpu.VMEM((tm, D_FF), BF16)],
        compiler_params=_cparams(1),
        name="mix_ffn",
    )(att, att, att, dno, dno, dno, z, z, z, x2, x2, x2, gn, mod3, g2, w_o, w_up, conv_w, conv_b, w_down)


def _trunk(x, mod, p):
    batch, seq, d = x.shape
    assert seq % TOKEN_TILE == 0 and seq % GRID_W == 0
    x2 = x.reshape(batch * seq, d)
    mod3 = mod.reshape(batch, 6, d)
    att, dn, z, gates = _inproj_call(x2, mod3, p["norm1_g"], p["w_in"], p["dn_conv_w"], p["qg"], p["kg"],
                                     p["ones_att"], p["ones_dn"], p["alog_l"], p["dtb_l"], seq)
    att_o = _natten_call(att, p["bias_tbl"], batch, seq)
    dn_o = _gdn_call(dn, gates, batch, seq)
    y = _mix_ffn_call(att_o.reshape(batch * seq, ATT_W), dn_o.reshape(batch * seq, DN_W), z,
                      p["dn_out_norm"], x2, mod3, p["norm2_g"], p["w_o"], p["ffn_w_up"],
                      p["ffn_conv_w"], p["ffn_conv_b"], p["ffn_w_down"], seq)
    return y.reshape(batch, seq, d)


def kernel(x_prompt, x_sample, c_prompt, c_sample, ada_w, ada_b, norm1_g, norm2_g, w_in,
           att_q_norm, att_k_norm, att_rpb, dn_conv_w, dn_a_log, dn_dt_bias, dn_out_norm, w_o,
           ffn_w_up, ffn_conv_w, ffn_conv_b, ffn_w_down):
    depth = ada_w.shape[0]
    bp, bs = c_prompt.shape[0], c_sample.shape[0]
    n_c = -(-(bp + bs) // SUBLANES) * SUBLANES
    gate_lanes = lambda a: jnp.pad(a.reshape(1, N_DIR * N_DN_HEADS),
                                   ((0, 0), (N_DIR * N_DN_HEADS, GATE_PAD - N_GATE)))
    blk = np.arange(LANES) // ATT_HD
    ones_bd = jnp.asarray(blk[:, None] == blk[None, :], BF16)
    xs = [x_prompt, x_sample]
    for l in range(depth):
        p = {
            "norm1_g": norm1_g[l].reshape(1, -1), "norm2_g": norm2_g[l].reshape(1, -1),
            "w_in": jnp.pad(w_in[l], ((0, 0), (0, GATE_PAD - N_GATE))).astype(BF16),
            "alog_l": gate_lanes(dn_a_log[l]), "dtb_l": gate_lanes(dn_dt_bias[l]),
            "bias_tbl": _natten_bias_table(att_rpb[l]),
            "qg": jnp.tile(att_q_norm[l], N_ATT_HEADS).reshape(1, ATT_W),
            "kg": jnp.tile(att_k_norm[l], N_ATT_HEADS).reshape(1, ATT_W),
            "ones_att": ones_bd, "ones_dn": jnp.ones((LANES, LANES), BF16),
            "dn_conv_w": dn_conv_w[l], "dn_out_norm": dn_out_norm[l].reshape(1, -1),
            "w_o": w_o[l].astype(BF16), "ffn_w_up": ffn_w_up[l].astype(BF16),
            "ffn_conv_w": ffn_conv_w[l], "ffn_conv_b": ffn_conv_b[l].reshape(1, -1),
            "ffn_w_down": ffn_w_down[l].astype(BF16),
        }
        c_all = jnp.pad(jnp.concatenate([c_prompt, c_sample], axis=0), ((0, n_c - bp - bs), (0, 0)))
        mod = _mod_call(c_all, ada_w[l], ada_b[l])
        xs = [_trunk(xs[0], mod[:bp], p), _trunk(xs[1], mod[bp:bp + bs], p)]
    return tuple(xs)
```

```python
import functools

import numpy as np
import jax
import jax.numpy as jnp
from jax import lax
from jax.experimental import pallas as pl
from jax.experimental.pallas import tpu as pltpu

F32 = jnp.float32
BF16 = jnp.bfloat16

D_MODEL = 1024
GRID_W = 64
N_ATT_HEADS = 8
ATT_HD = 64
ATT_W = N_ATT_HEADS * ATT_HD
WIN_R = 8
WIN_C = 16
N_DN_HEADS = 4
DN_HD = 128
DN_W = N_DN_HEADS * DN_HD
N_DIR = 2
CHUNK = 64
D_FF = 2816
NORM_EPS = 1e-6
N_GATE = 2 * N_DIR * N_DN_HEADS
LANES = 128
GATE_PAD = LANES
IN_COLS_PAD = 3 * ATT_W + 4 * DN_W + GATE_PAD
MASK_NEG = -1e30

VMEM_LIMIT = 56 * 1024 * 1024
TOKEN_TILE = 512
MOD_BLOCK = 1024
FFN_CHUNK = 256
FFN_AHEAD = 3
FFN_DOWN_SPLITS = (9, D_FF // FFN_CHUNK)
assert FFN_DOWN_SPLITS[-1] == D_FF // FFN_CHUNK
HALO = 16
IN_CHUNK = 512
assert ATT_W == IN_CHUNK and DN_W == IN_CHUNK
SUBLANES = 8
AQ_ROWS = DN_HD + CHUNK
NATTEN_UNROLL = 16
GDN_UNROLL = 16
GDN_STAGES = 15


def _cparams(n_axes):
    return pltpu.CompilerParams(dimension_semantics=("arbitrary",) * n_axes,
                                vmem_limit_bytes=VMEM_LIMIT)


def _const_spec(shape):
    nd = len(shape)
    return pl.BlockSpec(shape, lambda *_: (0,) * nd, pipeline_mode=pl.Buffered(1))


def _silu(x):
    return x * jax.nn.sigmoid(x)


def _dot(a, b):
    return jnp.dot(a, b, preferred_element_type=F32)


def _dot_nt(a, b):
    return lax.dot_general(a, b, (((1,), (1,)), ((), ())), preferred_element_type=F32)


def _dot_tn(a, b):
    return lax.dot_general(a, b, (((0,), (0,)), ((), ())), preferred_element_type=F32)


def _mod_kernel(c_ref, w_ref, b_ref, o_ref):
    s = _silu(c_ref[...])
    o_ref[...] = jnp.dot(s, w_ref[...], preferred_element_type=F32,
                         precision=lax.Precision.HIGHEST) + b_ref[...]


def _mod_call(c, ada_w, ada_b):
    bp, d = c.shape
    n = ada_w.shape[1]
    bn = MOD_BLOCK
    return pl.pallas_call(
        _mod_kernel,
        grid=(n // bn,),
        in_specs=[pl.BlockSpec((bp, d), lambda j: (0, 0)),
                  pl.BlockSpec((d, bn), lambda j: (0, j)),
                  pl.BlockSpec((1, bn), lambda j: (0, j))],
        out_specs=pl.BlockSpec((bp, bn), lambda j: (0, j)),
        out_shape=jax.ShapeDtypeStruct((bp, n), F32),
        compiler_params=_cparams(1),
        name="mod",
    )(c, ada_w, ada_b.reshape(1, n))


def _inproj_kernel(x_ref, xp_ref, xn_ref, mod_ref, g_ref, w_ref, cw_ref, qg_ref, kg_ref,
                   ones_att_ref, ones_dn_ref, alog_ref, dtb_ref,
                   att_ref, dn_ref, z_ref, gate_ref, *, per):
    i = pl.program_id(0)
    tm = x_ref.shape[0]
    cw = IN_CHUNK

    def norm_mod(x):
        ms = jnp.mean(x * x, axis=-1, keepdims=True)
        y = x * lax.rsqrt(ms + NORM_EPS) * g_ref[...]
        return (y * (1.0 + mod_ref[0, 1:2, :]) + mod_ref[0, 0:1, :]).astype(BF16)

    zero = jnp.zeros((HALO, D_MODEL), BF16)
    hp = jnp.where(i % per != 0, norm_mod(xp_ref[...]), zero)
    hn = jnp.where(i % per != per - 1, norm_mod(xn_ref[...]), zero)
    hx = jnp.concatenate([hp, norm_mod(x_ref[...]), hn], axis=0)
    h = hx[HALO:HALO + tm]

    def group_scale(y, ones_ref, mean_div):
        outs = []
        for b in range(cw // LANES):
            yb = y[:, b * LANES:(b + 1) * LANES]
            ss = _dot((yb * yb).astype(BF16), ones_ref[...])
            outs.append(lax.rsqrt(ss * (1.0 / mean_div) + NORM_EPS))
        return jnp.concatenate(outs, axis=1)

    def att_qk(gain_ref, scale):
        def fin(u, o):
            r = u * group_scale(u, ones_att_ref, ATT_HD) * (gain_ref[...] * scale)
            att_ref[:, o:o + cw] = r.astype(BF16)
        return fin

    def att_v(u, o):
        att_ref[:, o:o + cw] = u.astype(BF16)

    def dn_conv(kind):
        def fin(u, o):
            wc = cw_ref[:, o:o + cw]
            prev = pltpu.roll(u, 1, 0)[HALO:HALO + tm]
            nxt = pltpu.roll(u, tm + 2 * HALO - 1, 0)[HALO:HALO + tm]
            y = _silu(prev * wc[0:1, :] + u[HALO:HALO + tm] * wc[1:2, :] + nxt * wc[2:3, :])
            if kind == "q":
                y = y * group_scale(y, ones_dn_ref, 1.0) * (DN_HD ** -0.5)
            elif kind == "k":
                y = y * group_scale(y, ones_dn_ref, 1.0)
            dn_ref[:, o:o + cw] = y.astype(BF16)
        return fin

    def z_out(u, o):
        z_ref[...] = u.astype(BF16)

    def gates(u, o):
        lane = lax.broadcasted_iota(jnp.int32, (1, GATE_PAD), 1)
        t = u + dtb_ref[...]
        softplus = jnp.maximum(t, 0.0) + jnp.log1p(jnp.exp(-jnp.abs(t)))
        decay = -jnp.exp(alog_ref[...]) * softplus
        gate_ref[...] = jnp.where(lane < N_DIR * N_DN_HEADS, jax.nn.sigmoid(u), decay)

    o_dn, o_z, o_g = 3 * ATT_W, 3 * ATT_W + 3 * DN_W, 3 * ATT_W + 4 * DN_W
    jobs = [(0, cw, h, 0, att_qk(qg_ref, ATT_HD ** -0.5)),
            (cw, cw, h, cw, att_qk(kg_ref, 1.0)),
            (2 * cw, cw, h, 2 * cw, att_v),
            (o_dn, cw, hx, 0, dn_conv("q")),
            (o_dn + cw, cw, hx, cw, dn_conv("k")),
            (o_dn + 2 * cw, cw, hx, 2 * cw, dn_conv("v")),
            (o_z, DN_W, h, 0, z_out),
            (o_g, GATE_PAD, h, 0, gates)]
    proj = lambda job: _dot(job[2], w_ref[:, job[0]:job[0] + job[1]])
    u_next = proj(jobs[0])
    for n, job in enumerate(jobs):
        u = u_next
        if n + 1 < len(jobs):
            u_next = proj(jobs[n + 1])
        job[4](u, job[3])


def _inproj_call(x2, mod3, g1, w_cat, conv_w, qg, kg, ones_att, ones_dn, alog_l, dtb_l, seq):
    n_tok = x2.shape[0]
    tm = TOKEN_TILE
    per = seq // tm
    hb = tm // HALO
    last = n_tok // HALO - 1
    tok = lambda i: (i, 0)
    return pl.pallas_call(
        functools.partial(_inproj_kernel, per=per),
        grid=(n_tok // tm,),
        in_specs=[pl.BlockSpec((tm, D_MODEL), tok),
                  pl.BlockSpec((HALO, D_MODEL), lambda i: (jnp.maximum(i * hb - 1, 0), 0)),
                  pl.BlockSpec((HALO, D_MODEL), lambda i: (jnp.minimum((i + 1) * hb, last), 0)),
                  pl.BlockSpec((1, 6, D_MODEL), lambda i: (i // per, 0, 0)),
                  _const_spec((1, D_MODEL)),
                  _const_spec((D_MODEL, IN_COLS_PAD)),
                  _const_spec((3, 3 * DN_W)),
                  _const_spec((1, IN_CHUNK)),
                  _const_spec((1, IN_CHUNK)),
                  _const_spec((LANES, LANES)),
                  _const_spec((LANES, LANES)),
                  _const_spec((1, GATE_PAD)),
                  _const_spec((1, GATE_PAD))],
        out_specs=[pl.BlockSpec((tm, 3 * ATT_W), tok),
                   pl.BlockSpec((tm, 3 * DN_W), tok),
                   pl.BlockSpec((tm, DN_W), tok),
                   pl.BlockSpec((tm, GATE_PAD), tok)],
        out_shape=[jax.ShapeDtypeStruct((n_tok, 3 * ATT_W), BF16),
                   jax.ShapeDtypeStruct((n_tok, 3 * DN_W), BF16),
                   jax.ShapeDtypeStruct((n_tok, DN_W), BF16),
                   jax.ShapeDtypeStruct((n_tok, GATE_PAD), F32)],
        compiler_params=_cparams(1),
        name="inproj",
    )(x2, x2, x2, mod3, g1, w_cat, conv_w, qg, kg, ones_att, ones_dn, alog_l, dtb_l)


def _natten_kernel(q_ref, k_ref, v_ref, bias_ref, o_ref, *, rows):
    lane = lax.broadcasted_iota(jnp.int32, (1, LANES), 1)
    first = lane < ATT_HD
    band = WIN_R * GRID_W

    def row_group(i, carry):
        units = []
        for j in range(NATTEN_UNROLL):
            r = i * NATTEN_UNROLL + j
            r0 = jnp.clip(r - WIN_R // 2, 0, rows - WIN_R)
            qs = pl.ds(pl.multiple_of(r * GRID_W, GRID_W), GRID_W)
            ks = pl.ds(pl.multiple_of(r0 * GRID_W, GRID_W), band)
            units.append((qs, ks, r - r0))
        scores = []
        for qs, ks, dd in units:
            q = q_ref[0, qs, :]
            q2 = jnp.concatenate([jnp.where(first, q, 0), jnp.where(first, 0, q)], axis=0)
            bias = bias_ref[:, dd].reshape(2 * GRID_W, band)
            scores.append(_dot_nt(q2, k_ref[0, ks, :]) + bias)
        probs = []
        for s in scores:
            p = jnp.exp(s - jnp.max(s, axis=-1, keepdims=True))
            probs.append((p.astype(BF16), jnp.sum(p, axis=-1, keepdims=True)))
        for (qs, ks, _), (p, l) in zip(units, probs):
            o2 = _dot(p, v_ref[0, ks, :]) / l
            o_ref[0, qs, :] = jnp.where(first, o2[:GRID_W], o2[GRID_W:]).astype(BF16)
        return carry

    lax.fori_loop(0, rows // NATTEN_UNROLL, row_group, 0)


def _natten_call(att, bias_tbl, batch, seq):
    rows = seq // GRID_W
    assert rows >= WIN_R and rows % NATTEN_UNROLL == 0
    att3 = att.reshape(batch, seq, 3 * ATT_W)
    n_pair = N_ATT_HEADS // 2
    blk = (1, seq, LANES)
    return pl.pallas_call(
        functools.partial(_natten_kernel, rows=rows),
        grid=(n_pair, batch),
        in_specs=[pl.BlockSpec(blk, lambda p, b: (b, 0, p)),
                  pl.BlockSpec(blk, lambda p, b: (b, 0, n_pair + p)),
                  pl.BlockSpec(blk, lambda p, b: (b, 0, 2 * n_pair + p)),
                  pl.BlockSpec((2, WIN_R, GRID_W, WIN_R * GRID_W), lambda p, b: (p, 0, 0, 0))],
        out_specs=pl.BlockSpec(blk, lambda p, b: (b, 0, p)),
        out_shape=jax.ShapeDtypeStruct((batch, seq, ATT_W), BF16),
        compiler_params=_cparams(2),
        name="natten",
    )(att3, att3, att3, bias_tbl)


def _natten_bias_table(rpb):
    n_row = 2 * WIN_R - 1
    per = 2 * GRID_W
    v = jnp.zeros((N_ATT_HEADS, n_row, per), F32)
    v = v.at[:, :, :WIN_C].set(rpb[:, :, WIN_C - 1:])
    v = v.at[:, :, per - (WIN_C - 1):].set(rpb[:, :, :WIN_C - 1])
    toep = jnp.tile(v, (1, 1, GRID_W))[:, :, :GRID_W * (per - 1)]
    toep = toep.reshape(N_ATT_HEADS, n_row, GRID_W, per - 1)[:, :, :, :GRID_W]
    qc = np.arange(GRID_W)[:, None]
    kc = np.arange(GRID_W)[None, :]
    c0 = np.clip(qc - WIN_C // 2, 0, GRID_W - WIN_C)
    valid = (kc >= c0) & (kc < c0 + WIN_C)
    toep = jnp.where(valid[None, None], toep, MASK_NEG)
    bands = [jnp.swapaxes(toep[:, WIN_R - 1 - dd:2 * WIN_R - 1 - dd], 1, 2) for dd in range(WIN_R)]
    tbl = jnp.stack(bands, axis=1)
    return tbl.reshape(N_ATT_HEADS, WIN_R, GRID_W, WIN_R * GRID_W)


def _gdn_kernel(q_ref, k_ref, v_ref, gate_ref, o_ref, aq_s, b_s, eg_s, o0_s, *, seq, n_units):
    s = pl.program_id(0)
    n_chunk = seq // CHUNK
    n_group = n_chunk // GDN_UNROLL
    head = jnp.minimum(s, n_units - 1) % N_DN_HEADS
    wslot = s % 2
    rslot = 1 - wslot
    C = CHUNK
    lane = lax.broadcasted_iota(jnp.int32, (1, LANES), 1)
    left = lane < C
    ri = lax.broadcasted_iota(jnp.int32, (C, LANES), 0)
    li = lax.broadcasted_iota(jnp.int32, (C, LANES), 1)
    ci = li & (C - 1)
    fwd_half = li < C
    lag = jnp.where(fwd_half, ri - ci, ci - ri)
    causal_p = lag >= 0
    strict_p = lag > 0
    eye_left = (li == ri)
    b16 = ((ci >> 4) == (ri >> 4)).astype(F32)
    b32 = ((ci >> 5) == (ri >> 5)).astype(F32)
    m16 = b16
    m32 = b32 - b16
    m64 = 1.0 - b32

    @pl.when(s == 0)
    def _():
        aq_s[1] = jnp.zeros(aq_s.shape[1:], aq_s.dtype)
        b_s[1] = jnp.zeros(b_s.shape[1:], b_s.dtype)
        eg_s[1] = jnp.zeros(eg_s.shape[1:], eg_s.dtype)
        o0_s[1] = jnp.zeros(o0_s.shape[1:], o0_s.dtype)

    def blockdiag(p16):
        return jnp.concatenate([jnp.where(left, p16, 0), jnp.where(left, 0, p16)], axis=0)

    def pmm(a_list, b_list):
        return [_dot(a.astype(BF16), blockdiag(b.astype(BF16))) for a, b in zip(a_list, b_list)]

    def lane_pick(x, col):
        return jnp.sum(jnp.where(lane == col, x, 0.0), axis=-1, keepdims=True)

    def chunk_prep(c):
        rows = slice(c * C, (c + 1) * C)
        q16 = q_ref[0, rows, :]
        k16 = k_ref[0, rows, :]
        q, k, v = q16.astype(F32), k16.astype(F32), v_ref[0, rows, :].astype(F32)
        gram = _dot_nt(jnp.concatenate([k16, q16], axis=0), k16)
        gkk = jnp.concatenate([gram[:C], gram[:C]], axis=1)
        gqk = jnp.concatenate([gram[C:], gram[C:]], axis=1)

        gat = gate_ref[0, rows, :]
        beta, gcol, grow, glast = [], [], [], []
        for d in range(N_DIR):
            beta.append(lane_pick(gat, d * N_DN_HEADS + head))
            g = lane_pick(gat, N_DIR * N_DN_HEADS + d * N_DN_HEADS + head)
            gb = jnp.broadcast_to(g, (C, LANES))
            inc = (ri <= ci) if d == 0 else (ri >= ci)
            gr = jnp.sum(jnp.where(inc, gb, 0.0), axis=0, keepdims=True)
            grow.append(gr)
            gcol.append(jnp.sum(jnp.where(eye_left, jnp.broadcast_to(gr, (C, LANES)), 0.0),
                                axis=-1, keepdims=True))
            glast.append(jnp.sum(g, axis=0, keepdims=True))
        beta_p = jnp.where(fwd_half, beta[0], beta[1])
        diff = jnp.where(fwd_half, gcol[0] - grow[0], gcol[1] - grow[1])
        decay = jnp.exp(jnp.where(causal_p, diff, MASK_NEG))
        l_p = jnp.where(strict_p, beta_p * gkk * decay, 0.0)
        intra = jnp.where(causal_p, gqk * decay, 0.0).astype(BF16)

        rhs, kd, qd = [], [], []
        for d in range(N_DIR):
            eg = jnp.exp(gcol[d])
            kb = k * beta[d]
            rhs.append(jnp.concatenate([v * beta[d], kb * eg], axis=1))
            kd.append((k * jnp.exp(glast[d] - gcol[d])).astype(BF16))
            qd.append((q * eg).astype(BF16))
            eg_s[wslot, d, c * SUBLANES:(c + 1) * SUBLANES, :] = jnp.broadcast_to(
                jnp.exp(glast[d]), (SUBLANES, LANES))
        return dict(l_p=l_p, rhs=jnp.concatenate(rhs, axis=0).astype(BF16), intra=intra, kd=kd, qd=qd, c=c)

    def chunk_parallel(g):
        preps = [chunk_prep(g * GDN_UNROLL + j) for j in range(GDN_UNROLL)]
        yield
        l_ps = [p["l_p"] for p in preps]
        add = lambda xs, ys: [x + y for x, y in zip(xs, ys)]
        n = [-l_p * m16 for l_p in l_ps]
        xp = n
        for _ in range(3):
            xp = pmm(xp, xp)
            yield
            nx = pmm(n, xp)
            yield
            n = add(add(n, xp), nx)
        for m in (m32, m64):
            cm = [l_p * m for l_p in l_ps]
            y = add(cm, pmm(n, cm))
            yield
            yn = pmm(y, n)
            yield
            n = [a - b - c for a, b, c in zip(n, y, yn)]

        uw = [(p["rhs"].astype(F32) + _dot(blockdiag(a.astype(BF16)), p["rhs"])).astype(BF16)
              for p, a in zip(preps, n)]
        yield
        iw = [_dot(blockdiag(p["intra"]), x) for p, x in zip(preps, uw)]
        yield
        kt = [[_dot_tn(p["kd"][d], x[d * C:(d + 1) * C]) for d in range(N_DIR)]
              for p, x in zip(preps, uw)]
        yield
        for p, w_iw, w_kt in zip(preps, iw, kt):
            c = p["c"]
            for d in range(N_DIR):
                b_s[wslot, d, c * DN_HD:(c + 1) * DN_HD, :] = w_kt[d][:, :LANES]
                aq_s[wslot, d, c * AQ_ROWS:c * AQ_ROWS + DN_HD, :] = (-w_kt[d][:, LANES:]).astype(BF16)
                aq_s[wslot, d, c * AQ_ROWS + DN_HD:(c + 1) * AQ_ROWS, :] = (
                    p["qd"][d].astype(F32) - w_iw[d * C:(d + 1) * C, LANES:]).astype(BF16)
            o0_s[wslot, c * C:(c + 1) * C, :] = w_iw[:C, :LANES] + w_iw[C:, :LANES]

    def recurrence_step(n, states):
        new_states = []
        for d in range(N_DIR):
            c = n if d == 0 else n_chunk - 1 - n
            st = states[d]
            r = _dot(aq_s[rslot, d, c * AQ_ROWS:(c + 1) * AQ_ROWS, :], st.astype(BF16))
            rows = slice(c * C, (c + 1) * C)
            seen = o0_s[rslot, rows, :] if n < n_chunk // 2 else o_ref[0, rows, :]
            o_ref[0, rows, :] = seen + r[DN_HD:]
            new_states.append(st * eg_s[rslot, d, c * SUBLANES:c * SUBLANES + 1, :] + r[:DN_HD]
                              + b_s[rslot, d, c * DN_HD:(c + 1) * DN_HD, :])
        return new_states

    states = [jnp.zeros((DN_HD, DN_HD), F32)] * N_DIR
    for g in range(n_group):
        gen = chunk_parallel(g)
        done = 0
        for k in range(GDN_STAGES):
            next(gen, None)
            while done < (k + 1) * GDN_UNROLL // GDN_STAGES:
                states = recurrence_step(g * GDN_UNROLL + done, states)
                done += 1
        assert next(gen, "end") == "end" and done == GDN_UNROLL


def _gdn_call(dn, gates, batch, seq):
    dn3 = dn.reshape(batch, seq, 3 * DN_W)
    g3 = gates.reshape(batch, seq, GATE_PAD)
    blk = (1, seq, LANES)
    nh = N_DN_HEADS
    n_chunk = seq // CHUNK
    n_units = batch * nh
    assert seq % (CHUNK * GDN_UNROLL) == 0
    unit = lambda s: jnp.minimum(s, n_units - 1)
    prev = lambda s: jnp.maximum(s - 1, 0)
    return pl.pallas_call(
        functools.partial(_gdn_kernel, seq=seq, n_units=n_units),
        grid=(n_units + 1,),
        in_specs=[pl.BlockSpec(blk, lambda s: (unit(s) // nh, 0, unit(s) % nh)),
                  pl.BlockSpec(blk, lambda s: (unit(s) // nh, 0, nh + unit(s) % nh)),
                  pl.BlockSpec(blk, lambda s: (unit(s) // nh, 0, 2 * nh + unit(s) % nh)),
                  pl.BlockSpec(blk, lambda s: (unit(s) // nh, 0, 0))],
        out_specs=pl.BlockSpec(blk, lambda s: (prev(s) // nh, 0, prev(s) % nh)),
        out_shape=jax.ShapeDtypeStruct((batch, seq, DN_W), F32),
        scratch_shapes=[
            pltpu.VMEM((2, N_DIR, AQ_ROWS * n_chunk, LANES), BF16),
            pltpu.VMEM((2, N_DIR, DN_HD * n_chunk, LANES), F32),
            pltpu.VMEM((2, N_DIR, SUBLANES * n_chunk, LANES), F32),
            pltpu.VMEM((2, seq, LANES), F32)],
        compiler_params=_cparams(1),
        name="gdn",
    )(dn3, dn3, dn3, g3)


def _mix_ffn_kernel(att_ref, attp_ref, attn_ref, dn_ref, dnp_ref, dnn_ref, z_ref, zp_ref, zn_ref,
                    x_ref, xp_ref, xn_ref, gn_ref, mod_ref, g_ref, wo_ref, wu_ref, cw_ref, cb_ref,
                    wd_ref, y_ref, u_s, act_s, *, per):
    i = pl.program_id(0)
    tm = x_ref.shape[0]
    rows = lambda p_ref, m_ref, n_ref: jnp.concatenate([p_ref[...], m_ref[...], n_ref[...]], axis=0)
    att, dn_raw, z, x = (rows(attp_ref, att_ref, attn_ref), rows(dnp_ref, dn_ref, dnn_ref),
                         rows(zp_ref, z_ref, zn_ref), rows(xp_ref, x_ref, xn_ref))
    heads = []
    for hd in range(N_DN_HEADS):
        cols = slice(hd * DN_HD, (hd + 1) * DN_HD)
        o = dn_raw[:, cols]
        o = o * lax.rsqrt(jnp.mean(o * o, axis=-1, keepdims=True) + NORM_EPS) * gn_ref[...]
        heads.append((o * _silu(z[:, cols].astype(F32))).astype(BF16))
    mix = _dot(att, wo_ref[0:ATT_W, :]) + _dot(jnp.concatenate(heads, axis=1), wo_ref[ATT_W:, :])
    x1 = x + mod_ref[0, 2:3, :] * mix
    ms = jnp.mean(x1 * x1, axis=-1, keepdims=True)
    hx = x1 * lax.rsqrt(ms + NORM_EPS) * g_ref[...]
    hx = (hx * (1.0 + mod_ref[0, 4:5, :]) + mod_ref[0, 3:4, :]).astype(BF16)
    row = lax.broadcasted_iota(jnp.int32, (tm + 2 * HALO, 1), 0)
    lo = jnp.where(i % per == 0, HALO, 0)
    hi = jnp.where(i % per == per - 1, HALO + tm, 2 * HALO + tm)
    hx = jnp.where((row < lo) | (row >= hi), jnp.zeros_like(hx), hx)
    x1 = x1[HALO:HALO + tm]
    n_chunk = D_FF // FFN_CHUNK

    def up(j):
        for half, o in enumerate((j * FFN_CHUNK, D_FF + j * FFN_CHUNK)):
            u_s[j % (FFN_AHEAD + 1), half] = _dot(hx, wu_ref[:, o:o + FFN_CHUNK])

    def gate(j):
        halves = []
        for half, o in enumerate((j * FFN_CHUNK, D_FF + j * FFN_CHUNK)):
            cw = cw_ref[:, o:o + FFN_CHUNK]
            taps = [u_s[j % (FFN_AHEAD + 1), half, HALO - 1 + k:HALO - 1 + k + tm, :] for k in range(3)]
            halves.append(taps[0] * cw[0:1, :] + taps[1] * cw[1:2, :] + taps[2] * cw[2:3, :]
                          + cb_ref[:, o:o + FFN_CHUNK])
        act_s[:, j * FFN_CHUNK:(j + 1) * FFN_CHUNK] = (_silu(halves[0]) * halves[1]).astype(BF16)

    acc = None
    start = 0
    for j in range(FFN_AHEAD):
        up(j)
    for j in range(n_chunk):
        if j + FFN_AHEAD < n_chunk:
            up(j + FFN_AHEAD)
        gate(j)
        if j + 1 in FFN_DOWN_SPLITS:
            lo, hi = start * FFN_CHUNK, (j + 1) * FFN_CHUNK
            part = _dot(act_s[:, lo:hi], wd_ref[lo:hi, :])
            acc = part if acc is None else acc + part
            start = j + 1
    y_ref[...] = x1 + mod_ref[0, 5:6, :] * acc


def _mix_ffn_call(att, dno, z, gn, x2, mod3, g2, w_o, w_up, conv_w, conv_b, w_down, seq):
    n_tok = x2.shape[0]
    tm = TOKEN_TILE
    per = seq // tm
    hb = tm // HALO
    last = n_tok // HALO - 1
    tok = lambda i: (i, 0)
    prev = lambda i: (jnp.maximum(i * hb - 1, 0), 0)
    nxt = lambda i: (jnp.minimum((i + 1) * hb, last), 0)
    halo3 = lambda width: [pl.BlockSpec((tm, width), tok), pl.BlockSpec((HALO, width), prev),
                           pl.BlockSpec((HALO, width), nxt)]
    return pl.pallas_call(
        functools.partial(_mix_ffn_kernel, per=per),
        grid=(n_tok // tm,),
        in_specs=halo3(ATT_W) + halo3(DN_W) + halo3(DN_W) + halo3(D_MODEL) + [
            _const_spec((1, DN_HD)),
            pl.BlockSpec((1, 6, D_MODEL), lambda i: (i // per, 0, 0)),
            _const_spec((1, D_MODEL)),
            _const_spec((D_MODEL, D_MODEL)),
            _const_spec((D_MODEL, 2 * D_FF)),
            _const_spec((3, 2 * D_FF)),
            _const_spec((1, 2 * D_FF)),
            _const_spec((D_FF, D_MODEL))],
        out_specs=pl.BlockSpec((tm, D_MODEL), tok),
        out_shape=jax.ShapeDtypeStruct((n_tok, D_MODEL), F32),
        scratch_shapes=[pltpu.VMEM((FFN_AHEAD + 1, 2, tm + 2 * HALO, FFN_CHUNK), F32),
                        pltpu.VMEM((tm, D_FF), BF16)],
        compiler_params=_cparams(1),
        name="mix_ffn",
    )(att, att, att, dno, dno, dno, z, z, z, x2, x2, x2, gn, mod3, g2, w_o, w_up, conv_w, conv_b, w_down)


def _trunk(x, mod, p):
    batch, seq, d = x.shape
    assert seq % TOKEN_TILE == 0 and seq % GRID_W == 0
    x2 = x.reshape(batch * seq, d)
    mod3 = mod.reshape(batch, 6, d)
    att, dn, z, gates = _inproj_call(x2, mod3, p["norm1_g"], p["w_in"], p["dn_conv_w"], p["qg"], p["kg"],
                                     p["ones_att"], p["ones_dn"], p["alog_l"], p["dtb_l"], seq)
    att_o = _natten_call(att, p["bias_tbl"], batch, seq)
    dn_o = _gdn_call(dn, gates, batch, seq)
    y = _mix_ffn_call(att_o.reshape(batch * seq, ATT_W), dn_o.reshape(batch * seq, DN_W), z,
                      p["dn_out_norm"], x2, mod3, p["norm2_g"], p["w_o"], p["ffn_w_up"],
                      p["ffn_conv_w"], p["ffn_conv_b"], p["ffn_w_down"], seq)
    return y.reshape(batch, seq, d)


def kernel(x_prompt, x_sample, c_prompt, c_sample, ada_w, ada_b, norm1_g, norm2_g, w_in,
           att_q_norm, att_k_norm, att_rpb, dn_conv_w, dn_a_log, dn_dt_bias, dn_out_norm, w_o,
           ffn_w_up, ffn_conv_w, ffn_conv_b, ffn_w_down):
    depth = ada_w.shape[0]
    bp, bs = c_prompt.shape[0], c_sample.shape[0]
    n_c = -(-(bp + bs) // SUBLANES) * SUBLANES
    gate_lanes = lambda a: jnp.pad(a.reshape(1, N_DIR * N_DN_HEADS),
                                   ((0, 0), (N_DIR * N_DN_HEADS, GATE_PAD - N_GATE)))
    blk = np.arange(LANES) // ATT_HD
    ones_bd = jnp.asarray(blk[:, None] == blk[None, :], BF16)
    xs = [x_prompt, x_sample]
    for l in range(depth):
        p = {
            "norm1_g": norm1_g[l].reshape(1, -1), "norm2_g": norm2_g[l].reshape(1, -1),
            "w_in": jnp.pad(w_in[l], ((0, 0), (0, GATE_PAD - N_GATE))).astype(BF16),
            "alog_l": gate_lanes(dn_a_log[l]), "dtb_l": gate_lanes(dn_dt_bias[l]),
            "bias_tbl": _natten_bias_table(att_rpb[l]),
            "qg": jnp.tile(att_q_norm[l], N_ATT_HEADS).reshape(1, ATT_W),
            "kg": jnp.tile(att_k_norm[l], N_ATT_HEADS).reshape(1, ATT_W),
            "ones_att": ones_bd, "ones_dn": jnp.ones((LANES, LANES), BF16),
            "dn_conv_w": dn_conv_w[l], "dn_out_norm": dn_out_norm[l].reshape(1, -1),
            "w_o": w_o[l].astype(BF16), "ffn_w_up": ffn_w_up[l].astype(BF16),
            "ffn_conv_w": ffn_conv_w[l], "ffn_conv_b": ffn_conv_b[l].reshape(1, -1),
            "ffn_w_down": ffn_w_down[l].astype(BF16),
        }
        c_all = jnp.pad(jnp.concatenate([c_prompt, c_sample], axis=0), ((0, n_c - bp - bs), (0, 0)))
        mod = _mod_call(c_all, ada_w[l], ada_b[l])
        xs = [_trunk(xs[0], mod[:bp], p), _trunk(xs[1], mod[bp:bp + bs], p)]
    return tuple(xs)
```

```python
import functools

import numpy as np
import jax
import jax.numpy as jnp
from jax import lax
from jax.experimental import pallas as pl
from jax.experimental.pallas import tpu as pltpu

F32 = jnp.float32
BF16 = jnp.bfloat16

D_MODEL = 1024
GRID_W = 64
N_ATT_HEADS = 8
ATT_HD = 64
ATT_W = N_ATT_HEADS * ATT_HD
WIN_R = 8
WIN_C = 16
N_DN_HEADS = 4
DN_HD = 128
DN_W = N_DN_HEADS * DN_HD
N_DIR = 2
CHUNK = 64
D_FF = 2816
NORM_EPS = 1e-6
N_GATE = 2 * N_DIR * N_DN_HEADS
LANES = 128
GATE_PAD = LANES
IN_COLS_PAD = 3 * ATT_W + 4 * DN_W + GATE_PAD
MASK_NEG = -1e30

VMEM_LIMIT = 56 * 1024 * 1024
TOKEN_TILE = 512
MOD_BLOCK = 1024
FFN_CHUNK = 256
FFN_DOWN_SPLITS = (9, D_FF // FFN_CHUNK)
assert FFN_DOWN_SPLITS[-1] == D_FF // FFN_CHUNK
HALO = 16
IN_CHUNK = 512
assert ATT_W == IN_CHUNK and DN_W == IN_CHUNK
SUBLANES = 8
AQ_ROWS = DN_HD + CHUNK
NATTEN_UNROLL = 16
GDN_UNROLL = 16
GDN_STAGES = 15


def _cparams(n_axes):
    return pltpu.CompilerParams(dimension_semantics=("arbitrary",) * n_axes,
                                vmem_limit_bytes=VMEM_LIMIT)


def _const_spec(shape):
    nd = len(shape)
    return pl.BlockSpec(shape, lambda *_: (0,) * nd, pipeline_mode=pl.Buffered(1))


def _silu(x):
    return x * jax.nn.sigmoid(x)


def _dot(a, b):
    return jnp.dot(a, b, preferred_element_type=F32)


def _dot_nt(a, b):
    return lax.dot_general(a, b, (((1,), (1,)), ((), ())), preferred_element_type=F32)


def _dot_tn(a, b):
    return lax.dot_general(a, b, (((0,), (0,)), ((), ())), preferred_element_type=F32)


def _mod_kernel(c_ref, w_ref, b_ref, o_ref):
    s = _silu(c_ref[...])
    o_ref[...] = jnp.dot(s, w_ref[...], preferred_element_type=F32,
                         precision=lax.Precision.HIGHEST) + b_ref[...]


def _mod_call(c, ada_w, ada_b):
    bp, d = c.shape
    n = ada_w.shape[1]
    bn = MOD_BLOCK
    return pl.pallas_call(
        _mod_kernel,
        grid=(n // bn,),
        in_specs=[pl.BlockSpec((bp, d), lambda j: (0, 0)),
                  pl.BlockSpec((d, bn), lambda j: (0, j)),
                  pl.BlockSpec((1, bn), lambda j: (0, j))],
        out_specs=pl.BlockSpec((bp, bn), lambda j: (0, j)),
        out_shape=jax.ShapeDtypeStruct((bp, n), F32),
        compiler_params=_cparams(1),
        name="mod",
    )(c, ada_w, ada_b.reshape(1, n))


def _inproj_kernel(x_ref, xp_ref, xn_ref, mod_ref, g_ref, w_ref, cw_ref, qg_ref, kg_ref,
                   alog_ref, dtb_ref, att_ref, dn_ref, z_ref, gate_ref, *, per):
    i = pl.program_id(0)
    tm = x_ref.shape[0]
    cw = IN_CHUNK

    def norm_mod(x):
        ms = jnp.mean(x * x, axis=-1, keepdims=True)
        y = x * lax.rsqrt(ms + NORM_EPS) * g_ref[...]
        return (y * (1.0 + mod_ref[0, 1:2, :]) + mod_ref[0, 0:1, :]).astype(BF16)

    zero = jnp.zeros((HALO, D_MODEL), BF16)
    hp = jnp.where(i % per != 0, norm_mod(xp_ref[...]), zero)
    hn = jnp.where(i % per != per - 1, norm_mod(xn_ref[...]), zero)
    hx = jnp.concatenate([hp, norm_mod(x_ref[...]), hn], axis=0)
    h = hx[HALO:HALO + tm]

    first_half = lax.broadcasted_iota(jnp.int32, (1, LANES), 1) < LANES // 2

    def group_scale(y, width, mean_div):
        outs = []
        for b in range(cw // LANES):
            yb = y[:, b * LANES:(b + 1) * LANES]
            sq = yb * yb
            if width == LANES:
                ss = jnp.sum(sq, axis=-1, keepdims=True)
            else:
                lo = jnp.sum(jnp.where(first_half, sq, 0.0), axis=-1, keepdims=True)
                hi = jnp.sum(jnp.where(first_half, 0.0, sq), axis=-1, keepdims=True)
                ss = jnp.where(first_half, lo, hi)
            outs.append(jnp.broadcast_to(lax.rsqrt(ss * (1.0 / mean_div) + NORM_EPS), yb.shape))
        return jnp.concatenate(outs, axis=1)

    def att_qk(gain_ref, scale):
        def fin(u, o):
            r = u * group_scale(u, ATT_HD, ATT_HD) * (gain_ref[...] * scale)
            att_ref[:, o:o + cw] = r.astype(BF16)
        return fin

    def att_v(u, o):
        att_ref[:, o:o + cw] = u.astype(BF16)

    def dn_conv(kind):
        def fin(u, o):
            wc = cw_ref[:, o:o + cw]
            prev = pltpu.roll(u, 1, 0)[HALO:HALO + tm]
            nxt = pltpu.roll(u, tm + 2 * HALO - 1, 0)[HALO:HALO + tm]
            y = _silu(prev * wc[0:1, :] + u[HALO:HALO + tm] * wc[1:2, :] + nxt * wc[2:3, :])
            if kind == "q":
                y = y * group_scale(y, DN_HD, 1.0) * (DN_HD ** -0.5)
            elif kind == "k":
                y = y * group_scale(y, DN_HD, 1.0)
            dn_ref[:, o:o + cw] = y.astype(BF16)
        return fin

    def z_out(u, o):
        z_ref[...] = u.astype(BF16)

    def gates(u, o):
        lane = lax.broadcasted_iota(jnp.int32, (1, GATE_PAD), 1)
        t = u + dtb_ref[...]
        softplus = jnp.maximum(t, 0.0) + jnp.log1p(jnp.exp(-jnp.abs(t)))
        decay = -jnp.exp(alog_ref[...]) * softplus
        gate_ref[...] = jnp.where(lane < N_DIR * N_DN_HEADS, jax.nn.sigmoid(u), decay)

    o_dn, o_z, o_g = 3 * ATT_W, 3 * ATT_W + 3 * DN_W, 3 * ATT_W + 4 * DN_W
    jobs = [(0, cw, h, 0, att_qk(qg_ref, ATT_HD ** -0.5)),
            (cw, cw, h, cw, att_qk(kg_ref, 1.0)),
            (2 * cw, cw, h, 2 * cw, att_v),
            (o_dn, cw, hx, 0, dn_conv("q")),
            (o_dn + cw, cw, hx, cw, dn_conv("k")),
            (o_dn + 2 * cw, cw, hx, 2 * cw, dn_conv("v")),
            (o_z, DN_W, h, 0, z_out),
            (o_g, GATE_PAD, h, 0, gates)]
    proj = lambda job: _dot(job[2], w_ref[:, job[0]:job[0] + job[1]])
    u_next = proj(jobs[0])
    for n, job in enumerate(jobs):
        u = u_next
        if n + 1 < len(jobs):
            u_next = proj(jobs[n + 1])
        job[4](u, job[3])


def _inproj_call(x2, mod3, g1, w_cat, conv_w, qg, kg, alog_l, dtb_l, seq):
    n_tok = x2.shape[0]
    tm = TOKEN_TILE
    per = seq // tm
    hb = tm // HALO
    last = n_tok // HALO - 1
    tok = lambda i: (i, 0)
    return pl.pallas_call(
        functools.partial(_inproj_kernel, per=per),
        grid=(n_tok // tm,),
        in_specs=[pl.BlockSpec((tm, D_MODEL), tok),
                  pl.BlockSpec((HALO, D_MODEL), lambda i: (jnp.maximum(i * hb - 1, 0), 0)),
                  pl.BlockSpec((HALO, D_MODEL), lambda i: (jnp.minimum((i + 1) * hb, last), 0)),
                  pl.BlockSpec((1, 6, D_MODEL), lambda i: (i // per, 0, 0)),
                  _const_spec((1, D_MODEL)),
                  _const_spec((D_MODEL, IN_COLS_PAD)),
                  _const_spec((3, 3 * DN_W)),
                  _const_spec((1, IN_CHUNK)),
                  _const_spec((1, IN_CHUNK)),
                  _const_spec((1, GATE_PAD)),
                  _const_spec((1, GATE_PAD))],
        out_specs=[pl.BlockSpec((tm, 3 * ATT_W), tok),
                   pl.BlockSpec((tm, 3 * DN_W), tok),
                   pl.BlockSpec((tm, DN_W), tok),
                   pl.BlockSpec((tm, GATE_PAD), tok)],
        out_shape=[jax.ShapeDtypeStruct((n_tok, 3 * ATT_W), BF16),
                   jax.ShapeDtypeStruct((n_tok, 3 * DN_W), BF16),
                   jax.ShapeDtypeStruct((n_tok, DN_W), BF16),
                   jax.ShapeDtypeStruct((n_tok, GATE_PAD), F32)],
        compiler_params=_cparams(1),
        name="inproj",
    )(x2, x2, x2, mod3, g1, w_cat, conv_w, qg, kg, alog_l, dtb_l)


def _natten_kernel(q_ref, k_ref, v_ref, bias_ref, o_ref, *, rows):
    lane = lax.broadcasted_iota(jnp.int32, (1, LANES), 1)
    first = lane < ATT_HD
    band = WIN_R * GRID_W

    def row_group(i, carry):
        units = []
        for j in range(NATTEN_UNROLL):
            r = i * NATTEN_UNROLL + j
            r0 = jnp.clip(r - WIN_R // 2, 0, rows - WIN_R)
            qs = pl.ds(pl.multiple_of(r * GRID_W, GRID_W), GRID_W)
            ks = pl.ds(pl.multiple_of(r0 * GRID_W, GRID_W), band)
            units.append((qs, ks, r - r0))
        scores = []
        for qs, ks, dd in units:
            q = q_ref[0, qs, :]
            q2 = jnp.concatenate([jnp.where(first, q, 0), jnp.where(first, 0, q)], axis=0)
            bias = bias_ref[:, dd].reshape(2 * GRID_W, band)
            scores.append(_dot_nt(q2, k_ref[0, ks, :]) + bias)
        probs = []
        for s in scores:
            p = jnp.exp(s - jnp.max(s, axis=-1, keepdims=True))
            probs.append((p.astype(BF16), jnp.sum(p, axis=-1, keepdims=True)))
        for (qs, ks, _), (p, l) in zip(units, probs):
            o2 = _dot(p, v_ref[0, ks, :]) / l
            o_ref[0, qs, :] = jnp.where(first, o2[:GRID_W], o2[GRID_W:]).astype(BF16)
        return carry

    lax.fori_loop(0, rows // NATTEN_UNROLL, row_group, 0)


def _natten_call(att, bias_tbl, batch, seq):
    rows = seq // GRID_W
    assert rows >= WIN_R and rows % NATTEN_UNROLL == 0
    att3 = att.reshape(batch, seq, 3 * ATT_W)
    n_pair = N_ATT_HEADS // 2
    blk = (1, seq, LANES)
    return pl.pallas_call(
        functools.partial(_natten_kernel, rows=rows),
        grid=(n_pair, batch),
        in_specs=[pl.BlockSpec(blk, lambda p, b: (b, 0, p)),
                  pl.BlockSpec(blk, lambda p, b: (b, 0, n_pair + p)),
                  pl.BlockSpec(blk, lambda p, b: (b, 0, 2 * n_pair + p)),
                  pl.BlockSpec((2, WIN_R, GRID_W, WIN_R * GRID_W), lambda p, b: (p, 0, 0, 0))],
        out_specs=pl.BlockSpec(blk, lambda p, b: (b, 0, p)),
        out_shape=jax.ShapeDtypeStruct((batch, seq, ATT_W), BF16),
        compiler_params=_cparams(2),
        name="natten",
    )(att3, att3, att3, bias_tbl)


def _natten_bias_table(rpb):
    n_row = 2 * WIN_R - 1
    per = 2 * GRID_W
    v = jnp.zeros((N_ATT_HEADS, n_row, per), F32)
    v = v.at[:, :, :WIN_C].set(rpb[:, :, WIN_C - 1:])
    v = v.at[:, :, per - (WIN_C - 1):].set(rpb[:, :, :WIN_C - 1])
    toep = jnp.tile(v, (1, 1, GRID_W))[:, :, :GRID_W * (per - 1)]
    toep = toep.reshape(N_ATT_HEADS, n_row, GRID_W, per - 1)[:, :, :, :GRID_W]
    qc = np.arange(GRID_W)[:, None]
    kc = np.arange(GRID_W)[None, :]
    c0 = np.clip(qc - WIN_C // 2, 0, GRID_W - WIN_C)
    valid = (kc >= c0) & (kc < c0 + WIN_C)
    toep = jnp.where(valid[None, None], toep, MASK_NEG)
    bands = [jnp.swapaxes(toep[:, WIN_R - 1 - dd:2 * WIN_R - 1 - dd], 1, 2) for dd in range(WIN_R)]
    tbl = jnp.stack(bands, axis=1)
    return tbl.reshape(N_ATT_HEADS, WIN_R, GRID_W, WIN_R * GRID_W)


def _gdn_kernel(q_ref, k_ref, v_ref, gate_ref, o_ref, aq_s, b_s, eg_s, o0_s, *, seq, n_units):
    s = pl.program_id(0)
    n_chunk = seq // CHUNK
    n_group = n_chunk // GDN_UNROLL
    head = jnp.minimum(s, n_units - 1) % N_DN_HEADS
    wslot = s % 2
    rslot = 1 - wslot
    C = CHUNK
    lane = lax.broadcasted_iota(jnp.int32, (1, LANES), 1)
    left = lane < C
    ri = lax.broadcasted_iota(jnp.int32, (C, LANES), 0)
    li = lax.broadcasted_iota(jnp.int32, (C, LANES), 1)
    ci = li & (C - 1)
    fwd_half = li < C
    lag = jnp.where(fwd_half, ri - ci, ci - ri)
    causal_p = lag >= 0
    strict_p = lag > 0
    eye_left = (li == ri)
    b16 = ((ci >> 4) == (ri >> 4)).astype(F32)
    b32 = ((ci >> 5) == (ri >> 5)).astype(F32)
    m16 = b16
    m32 = b32 - b16
    m64 = 1.0 - b32

    @pl.when(s == 0)
    def _():
        aq_s[1] = jnp.zeros(aq_s.shape[1:], aq_s.dtype)
        b_s[1] = jnp.zeros(b_s.shape[1:], b_s.dtype)
        eg_s[1] = jnp.zeros(eg_s.shape[1:], eg_s.dtype)
        o0_s[1] = jnp.zeros(o0_s.shape[1:], o0_s.dtype)

    def blockdiag(p16):
        return jnp.concatenate([jnp.where(left, p16, 0), jnp.where(left, 0, p16)], axis=0)

    def pmm(a_list, b_list):
        return [_dot(a.astype(BF16), blockdiag(b.astype(BF16))) for a, b in zip(a_list, b_list)]

    def lane_pick(x, col):
        return jnp.sum(jnp.where(lane == col, x, 0.0), axis=-1, keepdims=True)

    def chunk_prep(c):
        rows = slice(c * C, (c + 1) * C)
        q16 = q_ref[0, rows, :]
        k16 = k_ref[0, rows, :]
        q, k, v = q16.astype(F32), k16.astype(F32), v_ref[0, rows, :].astype(F32)
        gram = _dot_nt(jnp.concatenate([k16, q16], axis=0), k16)
        gkk = jnp.concatenate([gram[:C], gram[:C]], axis=1)
        gqk = jnp.concatenate([gram[C:], gram[C:]], axis=1)

        gat = gate_ref[0, rows, :]
        beta, gcol, grow, glast = [], [], [], []
        for d in range(N_DIR):
            beta.append(lane_pick(gat, d * N_DN_HEADS + head))
            g = lane_pick(gat, N_DIR * N_DN_HEADS + d * N_DN_HEADS + head)
            gb = jnp.broadcast_to(g, (C, LANES))
            inc = (ri <= ci) if d == 0 else (ri >= ci)
            gr = jnp.sum(jnp.where(inc, gb, 0.0), axis=0, keepdims=True)
            grow.append(gr)
            gcol.append(jnp.sum(jnp.where(eye_left, jnp.broadcast_to(gr, (C, LANES)), 0.0),
                                axis=-1, keepdims=True))
            glast.append(jnp.sum(g, axis=0, keepdims=True))
        beta_p = jnp.where(fwd_half, beta[0], beta[1])
        diff = jnp.where(fwd_half, gcol[0] - grow[0], gcol[1] - grow[1])
        decay = jnp.exp(jnp.where(causal_p, diff, MASK_NEG))
        l_p = jnp.where(strict_p, beta_p * gkk * decay, 0.0)
        intra = jnp.where(causal_p, gqk * decay, 0.0).astype(BF16)

        rhs, kd, qd = [], [], []
        for d in range(N_DIR):
            eg = jnp.exp(gcol[d])
            kb = k * beta[d]
            rhs.append(jnp.concatenate([v * beta[d], kb * eg], axis=1))
            kd.append((k * jnp.exp(glast[d] - gcol[d])).astype(BF16))
            qd.append((q * eg).astype(BF16))
            eg_s[wslot, d, c * SUBLANES:(c + 1) * SUBLANES, :] = jnp.broadcast_to(
                jnp.exp(glast[d]), (SUBLANES, LANES))
        return dict(l_p=l_p, rhs=jnp.concatenate(rhs, axis=0).astype(BF16), intra=intra, kd=kd, qd=qd, c=c)

    def chunk_parallel(g):
        preps = [chunk_prep(g * GDN_UNROLL + j) for j in range(GDN_UNROLL)]
        yield
        l_ps = [p["l_p"] for p in preps]
        add = lambda xs, ys: [x + y for x, y in zip(xs, ys)]
        n = [-l_p * m16 for l_p in l_ps]
        xp = n
        for _ in range(3):
            xp = pmm(xp, xp)
            yield
            nx = pmm(n, xp)
            yield
            n = add(add(n, xp), nx)
        for m in (m32, m64):
            cm = [l_p * m for l_p in l_ps]
            y = add(cm, pmm(n, cm))
            yield
            yn = pmm(y, n)
            yield
            n = [a - b - c for a, b, c in zip(n, y, yn)]

        uw = [(p["rhs"].astype(F32) + _dot(blockdiag(a.astype(BF16)), p["rhs"])).astype(BF16)
              for p, a in zip(preps, n)]
        yield
        iw = [_dot(blockdiag(p["intra"]), x) for p, x in zip(preps, uw)]
        yield
        kt = [[_dot_tn(p["kd"][d], x[d * C:(d + 1) * C]) for d in range(N_DIR)]
              for p, x in zip(preps, uw)]
        yield
        for p, w_iw, w_kt in zip(preps, iw, kt):
            c = p["c"]
            for d in range(N_DIR):
                b_s[wslot, d, c * DN_HD:(c + 1) * DN_HD, :] = w_kt[d][:, :LANES]
                aq_s[wslot, d, c * AQ_ROWS:c * AQ_ROWS + DN_HD, :] = (-w_kt[d][:, LANES:]).astype(BF16)
                aq_s[wslot, d, c * AQ_ROWS + DN_HD:(c + 1) * AQ_ROWS, :] = (
                    p["qd"][d].astype(F32) - w_iw[d * C:(d + 1) * C, LANES:]).astype(BF16)
            o0_s[wslot, c * C:(c + 1) * C, :] = w_iw[:C, :LANES] + w_iw[C:, :LANES]

    def recurrence_step(n, states):
        new_states = []
        for d in range(N_DIR):
            c = n if d == 0 else n_chunk - 1 - n
            st = states[d]
            r = _dot(aq_s[rslot, d, c * AQ_ROWS:(c + 1) * AQ_ROWS, :], st.astype(BF16))
            rows = slice(c * C, (c + 1) * C)
            seen = o0_s[rslot, rows, :] if n < n_chunk // 2 else o_ref[0, rows, :]
            o_ref[0, rows, :] = seen + r[DN_HD:]
            new_states.append(st * eg_s[rslot, d, c * SUBLANES:c * SUBLANES + 1, :] + r[:DN_HD]
                              + b_s[rslot, d, c * DN_HD:(c + 1) * DN_HD, :])
        return new_states

    states = [jnp.zeros((DN_HD, DN_HD), F32)] * N_DIR
    for g in range(n_group):
        gen = chunk_parallel(g)
        done = 0
        for k in range(GDN_STAGES):
            next(gen, None)
            while done < (k + 1) * GDN_UNROLL // GDN_STAGES:
                states = recurrence_step(g * GDN_UNROLL + done, states)
                done += 1
        assert next(gen, "end") == "end" and done == GDN_UNROLL


def _gdn_call(dn, gates, batch, seq):
    dn3 = dn.reshape(batch, seq, 3 * DN_W)
    g3 = gates.reshape(batch, seq, GATE_PAD)
    blk = (1, seq, LANES)
    nh = N_DN_HEADS
    n_chunk = seq // CHUNK
    n_units = batch * nh
    assert seq % (CHUNK * GDN_UNROLL) == 0
    unit = lambda s: jnp.minimum(s, n_units - 1)
    prev = lambda s: jnp.maximum(s - 1, 0)
    return pl.pallas_call(
        functools.partial(_gdn_kernel, seq=seq, n_units=n_units),
        grid=(n_units + 1,),
        in_specs=[pl.BlockSpec(blk, lambda s: (unit(s) // nh, 0, unit(s) % nh)),
                  pl.BlockSpec(blk, lambda s: (unit(s) // nh, 0, nh + unit(s) % nh)),
                  pl.BlockSpec(blk, lambda s: (unit(s) // nh, 0, 2 * nh + unit(s) % nh)),
                  pl.BlockSpec(blk, lambda s: (unit(s) // nh, 0, 0))],
        out_specs=pl.BlockSpec(blk, lambda s: (prev(s) // nh, 0, prev(s) % nh)),
        out_shape=jax.ShapeDtypeStruct((batch, seq, DN_W), F32),
        scratch_shapes=[
            pltpu.VMEM((2, N_DIR, AQ_ROWS * n_chunk, LANES), BF16),
            pltpu.VMEM((2, N_DIR, DN_HD * n_chunk, LANES), F32),
            pltpu.VMEM((2, N_DIR, SUBLANES * n_chunk, LANES), F32),
            pltpu.VMEM((2, seq, LANES), F32)],
        compiler_params=_cparams(1),
        name="gdn",
    )(dn3, dn3, dn3, g3)


def _mix_ffn_kernel(att_ref, attp_ref, attn_ref, dn_ref, dnp_ref, dnn_ref, z_ref, zp_ref, zn_ref,
                    x_ref, xp_ref, xn_ref, gn_ref, mod_ref, g_ref, wo_ref, wu_ref, cw_ref, cb_ref,
                    wd_ref, y_ref, u_s, act_s, *, per):
    i = pl.program_id(0)
    tm = x_ref.shape[0]
    rows = lambda p_ref, m_ref, n_ref: jnp.concatenate([p_ref[...], m_ref[...], n_ref[...]], axis=0)
    att, dn_raw, z, x = (rows(attp_ref, att_ref, attn_ref), rows(dnp_ref, dn_ref, dnn_ref),
                         rows(zp_ref, z_ref, zn_ref), rows(xp_ref, x_ref, xn_ref))
    heads = []
    for hd in range(N_DN_HEADS):
        cols = slice(hd * DN_HD, (hd + 1) * DN_HD)
        o = dn_raw[:, cols]
        o = o * lax.rsqrt(jnp.mean(o * o, axis=-1, keepdims=True) + NORM_EPS) * gn_ref[...]
        heads.append((o * _silu(z[:, cols].astype(F32))).astype(BF16))
    mix = _dot(att, wo_ref[0:ATT_W, :]) + _dot(jnp.concatenate(heads, axis=1), wo_ref[ATT_W:, :])
    x1 = x + mod_ref[0, 2:3, :] * mix
    ms = jnp.mean(x1 * x1, axis=-1, keepdims=True)
    hx = x1 * lax.rsqrt(ms + NORM_EPS) * g_ref[...]
    hx = (hx * (1.0 + mod_ref[0, 4:5, :]) + mod_ref[0, 3:4, :]).astype(BF16)
    row = lax.broadcasted_iota(jnp.int32, (tm + 2 * HALO, 1), 0)
    lo = jnp.where(i % per == 0, HALO, 0)
    hi = jnp.where(i % per == per - 1, HALO + tm, 2 * HALO + tm)
    hx = jnp.where((row < lo) | (row >= hi), jnp.zeros_like(hx), hx)
    x1 = x1[HALO:HALO + tm]
    n_chunk = D_FF // FFN_CHUNK

    def up(j):
        for half, o in enumerate((j * FFN_CHUNK, D_FF + j * FFN_CHUNK)):
            u_s[j % 2, half] = _dot(hx, wu_ref[:, o:o + FFN_CHUNK])

    def gate(j):
        halves = []
        for half, o in enumerate((j * FFN_CHUNK, D_FF + j * FFN_CHUNK)):
            cw = cw_ref[:, o:o + FFN_CHUNK]
            taps = [u_s[j % 2, half, HALO - 1 + k:HALO - 1 + k + tm, :] for k in range(3)]
            halves.append(taps[0] * cw[0:1, :] + taps[1] * cw[1:2, :] + taps[2] * cw[2:3, :]
                          + cb_ref[:, o:o + FFN_CHUNK])
        act_s[:, j * FFN_CHUNK:(j + 1) * FFN_CHUNK] = (_silu(halves[0]) * halves[1]).astype(BF16)

    acc = None
    start = 0
    up(0)
    for j in range(n_chunk):
        if j + 1 < n_chunk:
            up(j + 1)
        gate(j)
        if j + 1 in FFN_DOWN_SPLITS:
            lo, hi = start * FFN_CHUNK, (j + 1) * FFN_CHUNK
            part = _dot(act_s[:, lo:hi], wd_ref[lo:hi, :])
            acc = part if acc is None else acc + part
            start = j + 1
    y_ref[...] = x1 + mod_ref[0, 5:6, :] * acc


def _mix_ffn_call(att, dno, z, gn, x2, mod3, g2, w_o, w_up, conv_w, conv_b, w_down, seq):
    n_tok = x2.shape[0]
    tm = TOKEN_TILE
    per = seq // tm
    hb = tm // HALO
    last = n_tok // HALO - 1
    tok = lambda i: (i, 0)
    prev = lambda i: (jnp.maximum(i * hb - 1, 0), 0)
    nxt = lambda i: (jnp.minimum((i + 1) * hb, last), 0)
    halo3 = lambda width: [pl.BlockSpec((tm, width), tok), pl.BlockSpec((HALO, width), prev),
                           pl.BlockSpec((HALO, width), nxt)]
    return pl.pallas_call(
        functools.partial(_mix_ffn_kernel, per=per),
        grid=(n_tok // tm,),
        in_specs=halo3(ATT_W) + halo3(DN_W) + halo3(DN_W) + halo3(D_MODEL) + [
            _const_spec((1, DN_HD)),
            pl.BlockSpec((1, 6, D_MODEL), lambda i: (i // per, 0, 0)),
            _const_spec((1, D_MODEL)),
            _const_spec((D_MODEL, D_MODEL)),
            _const_spec((D_MODEL, 2 * D_FF)),
            _const_spec((3, 2 * D_FF)),
            _const_spec((1, 2 * D_FF)),
            _const_spec((D_FF, D_MODEL))],
        out_specs=pl.BlockSpec((tm, D_MODEL), tok),
        out_shape=jax.ShapeDtypeStruct((n_tok, D_MODEL), F32),
        scratch_shapes=[pltpu.VMEM((2, 2, tm + 2 * HALO, FFN_CHUNK), F32),
                        pltpu.VMEM((tm, D_FF), BF16)],
        compiler_params=_cparams(1),
        name="mix_ffn",
    )(att, att, att, dno, dno, dno, z, z, z, x2, x2, x2, gn, mod3, g2, w_o, w_up, conv_w, conv_b, w_down)


def _trunk(x, mod, p):
    batch, seq, d = x.shape
    assert seq % TOKEN_TILE == 0 and seq % GRID_W == 0
    x2 = x.reshape(batch * seq, d)
    mod3 = mod.reshape(batch, 6, d)
    att, dn, z, gates = _inproj_call(x2, mod3, p["norm1_g"], p["w_in"], p["dn_conv_w"], p["qg"], p["kg"],
                                     p["alog_l"], p["dtb_l"], seq)
    att_o = _natten_call(att, p["bias_tbl"], batch, seq)
    dn_o = _gdn_call(dn, gates, batch, seq)
    y = _mix_ffn_call(att_o.reshape(batch * seq, ATT_W), dn_o.reshape(batch * seq, DN_W), z,
                      p["dn_out_norm"], x2, mod3, p["norm2_g"], p["w_o"], p["ffn_w_up"],
                      p["ffn_conv_w"], p["ffn_conv_b"], p["ffn_w_down"], seq)
    return y.reshape(batch, seq, d)


def kernel(x_prompt, x_sample, c_prompt, c_sample, ada_w, ada_b, norm1_g, norm2_g, w_in,
           att_q_norm, att_k_norm, att_rpb, dn_conv_w, dn_a_log, dn_dt_bias, dn_out_norm, w_o,
           ffn_w_up, ffn_conv_w, ffn_conv_b, ffn_w_down):
    depth = ada_w.shape[0]
    bp, bs = c_prompt.shape[0], c_sample.shape[0]
    n_c = -(-(bp + bs) // SUBLANES) * SUBLANES
    gate_lanes = lambda a: jnp.pad(a.reshape(1, N_DIR * N_DN_HEADS),
                                   ((0, 0), (N_DIR * N_DN_HEADS, GATE_PAD - N_GATE)))
    xs = [x_prompt, x_sample]
    for l in range(depth):
        p = {
            "norm1_g": norm1_g[l].reshape(1, -1), "norm2_g": norm2_g[l].reshape(1, -1),
            "w_in": jnp.pad(w_in[l], ((0, 0), (0, GATE_PAD - N_GATE))).astype(BF16),
            "alog_l": gate_lanes(dn_a_log[l]), "dtb_l": gate_lanes(dn_dt_bias[l]),
            "bias_tbl": _natten_bias_table(att_rpb[l]),
            "qg": jnp.tile(att_q_norm[l], N_ATT_HEADS).reshape(1, ATT_W),
            "kg": jnp.tile(att_k_norm[l], N_ATT_HEADS).reshape(1, ATT_W),
            "dn_conv_w": dn_conv_w[l], "dn_out_norm": dn_out_norm[l].reshape(1, -1),
            "w_o": w_o[l].astype(BF16), "ffn_w_up": ffn_w_up[l].astype(BF16),
            "ffn_conv_w": ffn_conv_w[l], "ffn_conv_b": ffn_conv_b[l].reshape(1, -1),
            "ffn_w_down": ffn_w_down[l].astype(BF16),
        }
        c_all = jnp.pad(jnp.concatenate([c_prompt, c_sample], axis=0), ((0, n_c - bp - bs), (0, 0)))
        mod = _mod_call(c_all, ada_w[l], ada_b[l])
        xs = [_trunk(xs[0], mod[:bp], p), _trunk(xs[1], mod[bp:bp + bs], p)]
    return tuple(xs)
```
